```python
import jax, jax.numpy as jnp
from jax import lax
import numpy as np

D_MODEL = 1024
BATCH = 8
SEQ = 8192
DEPTH = 1

MIX_WIDTH = D_MODEL
RWKV_WIDTH = MIX_WIDTH // 2
POOL_WIDTH = MIX_WIDTH - RWKV_WIDTH
HEAD_SIZE = 64
N_HEADS = RWKV_WIDTH // HEAD_SIZE
D_DECAY_LORA = max(32, int(round(1.8 * RWKV_WIDTH ** 0.5 / 32)) * 32)
D_AAA_LORA = max(32, int(round(1.8 * RWKV_WIDTH ** 0.5 / 32)) * 32)
D_GATE_LORA = max(32, int(round(0.6 * RWKV_WIDTH ** 0.8 / 32)) * 32)
POOL_WINDOWS = (2, 4, 8, 16)
N_POOL_GROUPS = len(POOL_WINDOWS)
POOL_GROUP = POOL_WIDTH // N_POOL_GROUPS
D_FF = 256 * ((8 * D_MODEL // 3 + 255) // 256)
SPLIT_SIZES = (RWKV_WIDTH, RWKV_WIDTH, RWKV_WIDTH, D_DECAY_LORA, D_AAA_LORA, D_GATE_LORA, POOL_WIDTH)
RWKV_COLS = sum(SPLIT_SIZES[:6])
IN_COLS = sum(SPLIT_SIZES)
RMS_EPS = 1e-6
GN_EPS = 64e-5
L2_EPS = 1e-12

kernel_name = "hymba_rwkv7_multiscale_pool_macaron"


def rms_norm(x, g):
    xf = x.astype(jnp.float32)
    y = xf * lax.rsqrt(jnp.mean(xf * xf, axis=-1, keepdims=True) + RMS_EPS)
    return (y * g).astype(x.dtype)


def swiglu(x, w_gate, w_up, w_down):
    return (jax.nn.silu(x @ w_gate) * (x @ w_up)) @ w_down


def token_shift(p):
    return jnp.pad(p, ((0, 0), (1, 0), (0, 0)))[:, :-1]


def rwkv7_time_mix(p_r, p_k, p_v, p_w, p_a, p_g, w0, w_lora_up, a0, a_lora_up, g_lora_up,
                   k_k, k_a, r_k, ln_w, ln_b):
    B, S, _ = p_r.shape
    f32 = jnp.float32
    log_w = -jax.nn.softplus(-(w0 + jnp.tanh(p_w) @ w_lora_up).astype(f32)) - 0.5
    decay = jnp.exp(-jnp.exp(log_w))
    a = jax.nn.sigmoid((a0 + p_a @ a_lora_up).astype(f32))
    g = jax.nn.sigmoid(p_g) @ g_lora_up

    def heads(t):
        return t.reshape(B, S, N_HEADS, HEAD_SIZE).astype(f32)

    kk = heads(p_k * k_k)
    kk = kk / jnp.maximum(jnp.sqrt(jnp.sum(kk * kk, axis=-1, keepdims=True)), L2_EPS)
    k = heads(p_k * (1.0 + (a - 1.0) * k_a))
    r, v = heads(p_r), heads(p_v)
    w_h, a_h = heads(decay), heads(a)

    def step(state, inp):
        r_t, w_t, k_t, v_t, kk_t, a_t = inp
        sa = jnp.einsum('bhvk,bhk->bhv', state, -kk_t)
        state = (state * w_t[:, :, None, :]
                 + sa[..., None] * (kk_t * a_t)[:, :, None, :]
                 + v_t[..., None] * k_t[:, :, None, :])
        return state, jnp.einsum('bhvk,bhk->bhv', state, r_t)

    seq_major = lambda t: jnp.moveaxis(t, 1, 0)
    state0 = jnp.zeros((B, N_HEADS, HEAD_SIZE, HEAD_SIZE), f32)
    _, y = lax.scan(step, state0, (seq_major(r), seq_major(w_h), seq_major(k),
                                   seq_major(v), seq_major(kk), seq_major(a_h)))
    y = jnp.moveaxis(y, 0, 1)
    mean = jnp.mean(y, axis=-1, keepdims=True)
    var = jnp.mean(jnp.square(y - mean), axis=-1, keepdims=True)
    y = ((y - mean) * lax.rsqrt(var + GN_EPS)).reshape(B, S, RWKV_WIDTH) * ln_w + ln_b
    bonus = jnp.sum(r * k * r_k, axis=-1, keepdims=True) * v
    y = y + bonus.reshape(B, S, RWKV_WIDTH)
    return (y * g).astype(p_r.dtype)


def multiscale_pool_mix(p, w_pool, pool_scale):
    B, S, _ = p.shape
    f32 = jnp.float32
    pg = p.reshape(B, S, N_POOL_GROUPS, POOL_GROUP).astype(f32)
    cs = jnp.pad(jnp.cumsum(pg, axis=1), ((0, 0), (1, 0), (0, 0), (0, 0)))
    t1 = jnp.arange(1, S + 1)
    outs = []
    for gi, win in enumerate(POOL_WINDOWS):
        csg = cs[:, :, gi]
        lo = jnp.maximum(t1 - win, 0)
        window_sum = csg[:, 1:] - jnp.take(csg, lo, axis=1)
        count = jnp.minimum(t1, win).astype(f32)[None, :, None]
        outs.append(window_sum / count - pg[:, :, gi])
    pooled = jnp.stack(outs, axis=2)
    mixed = jnp.einsum('bsgc,gcd->bsgd', pooled, w_pool.astype(f32))
    return (mixed.reshape(B, S, POOL_WIDTH) * pool_scale).astype(p.dtype)


def setup_inputs(seed: int = 0) -> dict:
    key = jax.random.key(seed)
    ks = jax.random.split(key, 32)
    L = DEPTH
    f32 = jnp.float32

    def nrm(k, shape, scale):
        return jax.random.normal(k, shape, f32) * scale

    return {
        "x": nrm(ks[0], (BATCH, SEQ, D_MODEL), 1.0),
        "ffn1_norm": 1.0 + nrm(ks[1], (L, D_MODEL), 0.1),
        "ffn1_w_gate": nrm(ks[2], (L, D_MODEL, D_FF), D_MODEL ** -0.5),
        "ffn1_w_up": nrm(ks[3], (L, D_MODEL, D_FF), D_MODEL ** -0.5),
        "ffn1_w_down": nrm(ks[4], (L, D_FF, D_MODEL), D_FF ** -0.5),
        "mix_norm": 1.0 + nrm(ks[5], (L, D_MODEL), 0.1),
        "w_in": nrm(ks[6], (L, D_MODEL, IN_COLS), D_MODEL ** -0.5),
        "mu_shift": jax.random.uniform(ks[7], (L, RWKV_COLS), f32),
        "w0": jax.random.uniform(ks[8], (L, RWKV_WIDTH), f32, -6.0, -1.0),
        "w_lora_up": nrm(ks[9], (L, D_DECAY_LORA, RWKV_WIDTH), 0.5 * D_DECAY_LORA ** -0.5),
        "a0": nrm(ks[10], (L, RWKV_WIDTH), 0.1),
        "a_lora_up": nrm(ks[11], (L, D_AAA_LORA, RWKV_WIDTH), D_AAA_LORA ** -0.5),
        "g_lora_up": nrm(ks[12], (L, D_GATE_LORA, RWKV_WIDTH), D_GATE_LORA ** -0.5),
        "k_k": 0.85 + nrm(ks[13], (L, RWKV_WIDTH), 0.1),
        "k_a": 1.0 + nrm(ks[14], (L, RWKV_WIDTH), 0.1),
        "r_k": nrm(ks[15], (L, N_HEADS, HEAD_SIZE), 0.1),
        "ln_w": 1.0 + nrm(ks[16], (L, RWKV_WIDTH), 0.1),
        "ln_b": nrm(ks[17], (L, RWKV_WIDTH), 0.02),
        "w_pool": nrm(ks[18], (L, N_POOL_GROUPS, POOL_GROUP, POOL_GROUP), POOL_GROUP ** -0.5),
        "pool_scale": 1.0 + nrm(ks[19], (L, POOL_WIDTH), 0.1),
        "w_out": nrm(ks[20], (L, MIX_WIDTH, D_MODEL), MIX_WIDTH ** -0.5),
        "ffn2_norm": 1.0 + nrm(ks[21], (L, D_MODEL), 0.1),
        "ffn2_w_gate": nrm(ks[22], (L, D_MODEL, D_FF), D_MODEL ** -0.5),
        "ffn2_w_up": nrm(ks[23], (L, D_MODEL, D_FF), D_MODEL ** -0.5),
        "ffn2_w_down": nrm(ks[24], (L, D_FF, D_MODEL), D_FF ** -0.5),
        "final_norm": 1.0 + nrm(ks[25], (D_MODEL,), 0.1),
    }


def reference(x, ffn1_norm, ffn1_w_gate, ffn1_w_up, ffn1_w_down, mix_norm, w_in, mu_shift,
              w0, w_lora_up, a0, a_lora_up, g_lora_up, k_k, k_a, r_k, ln_w, ln_b,
              w_pool, pool_scale, w_out, ffn2_norm, ffn2_w_gate, ffn2_w_up, ffn2_w_down,
              final_norm):
    split_points = np.cumsum(SPLIT_SIZES[:5]).tolist()
    for l in range(DEPTH):
        x = x + 0.5 * swiglu(rms_norm(x, ffn1_norm[l]), ffn1_w_gate[l], ffn1_w_up[l], ffn1_w_down[l])

        h = rms_norm(x, mix_norm[l])
        p = h @ w_in[l]
        p_rw, p_pool = p[..., :RWKV_COLS], p[..., RWKV_COLS:]
        p_rw = p_rw + (token_shift(p_rw) - p_rw) * mu_shift[l]
        p_r, p_k, p_v, p_w, p_a, p_g = jnp.split(p_rw, split_points, axis=-1)
        y_rw = rwkv7_time_mix(p_r, p_k, p_v, p_w, p_a, p_g, w0[l], w_lora_up[l], a0[l],
                              a_lora_up[l], g_lora_up[l], k_k[l], k_a[l], r_k[l],
                              ln_w[l], ln_b[l])
        y_pool = multiscale_pool_mix(p_pool, w_pool[l], pool_scale[l])
        x = x + jnp.concatenate([y_rw, y_pool], axis=-1) @ w_out[l]

        x = x + 0.5 * swiglu(rms_norm(x, ffn2_norm[l]), ffn2_w_gate[l], ffn2_w_up[l], ffn2_w_down[l])
    return rms_norm(x, final_norm)
```

```python
import functools

import jax
import jax.numpy as jnp
from jax import lax
from jax.experimental import pallas as pl
from jax.experimental.pallas import tpu as pltpu

F32 = jnp.float32
BF16 = jnp.bfloat16

D_MODEL = 1024
RWKV_WIDTH = 512
POOL_WIDTH = 512
HEAD_SIZE = 64
D_DECAY_LORA = 32
D_AAA_LORA = 32
D_GATE_LORA = 96
LORA_COLS = D_DECAY_LORA + D_AAA_LORA + D_GATE_LORA
LORA_PAD = 256
POOL_WINDOWS = (2, 4, 8, 16)
POOL_GROUP = 128
POOL_HALO = 16
D_FF = 2816
RWKV_COLS = 3 * RWKV_WIDTH + LORA_COLS
SHIFT_COLS = 3 * RWKV_WIDTH + LORA_PAD
RMS_EPS = 1e-6
GN_EPS = 64e-5
L2_EPS = 1e-12

CHUNK = 64
GROUP_LANES = 256
HEADS_PER_GROUP = GROUP_LANES // HEAD_SIZE
N_GROUPS = RWKV_WIDTH // GROUP_LANES
FF_CHUNK = 1408

TOKEN_TILE = 256
SCAN_TILE = 256
VMEM_LIMIT = 56 * 1024 * 1024


def _dot(a, b):
    return jnp.dot(a, b, preferred_element_type=F32)


def _dot_nt(a, b):
    return lax.dot_general(a, b, (((1,), (1,)), ((), ())), preferred_element_type=F32)


def _dot_tn(a, b):
    return lax.dot_general(a, b, (((0,), (0,)), ((), ())), preferred_element_type=F32)


def _split_dot(x, w):
    hi = x.astype(BF16)
    lo = (x - hi.astype(F32)).astype(BF16)
    return _dot(hi, w) + _dot(lo, w)


def _rms(x, g):
    return x * lax.rsqrt(jnp.mean(x * x, axis=-1, keepdims=True) + RMS_EPS) * g


def _swiglu(hb, wg_ref, wu_ref, wd_ref):
    acc = None
    for c0 in range(0, D_FF, FF_CHUNK):
        gate = _dot(hb, wg_ref[:, c0:c0 + FF_CHUNK])
        up = _dot(hb, wu_ref[:, c0:c0 + FF_CHUNK])
        act = (gate * jax.nn.sigmoid(gate) * up).astype(BF16)
        part = _dot(act, wd_ref[c0:c0 + FF_CHUNK, :])
        acc = part if acc is None else acc + part
    return acc


def _pre_kernel(x_ref, n1_ref, wg_ref, wu_ref, wd_ref, nm_ref, win_ref, mu_ref, w0_ref,
                wlu_ref, a0_ref, alu_ref, glu_ref, kk_ref, ka_ref, rk_ref, wpool_ref,
                pscale_ref, hsum_ref,
                x1_ref, r_ref, k_ref, v_ref, kkn_ref, b_ref, ld_ref, g_ref, bonus_ref,
                ypool_ref, carry_ref, pool_ref):
    tm = x_ref.shape[0]
    t = pl.program_id(1)

    @pl.when(t == 0)
    def _():
        carry_ref[...] = jnp.zeros_like(carry_ref)
        pool_ref[0:POOL_HALO, :] = jnp.zeros((POOL_HALO, POOL_WIDTH), F32)

    x = x_ref[...]
    h = _rms(x, n1_ref[...]).astype(BF16)
    x1 = x + 0.5 * _swiglu(h, wg_ref, wu_ref, wd_ref)
    x1_ref[...] = x1

    h2 = _rms(x1, nm_ref[...]).astype(BF16)
    p = _dot(h2, win_ref[...])

    p_rw = p[:, :SHIFT_COLS]
    row = lax.broadcasted_iota(jnp.int32, (tm, SHIFT_COLS), 0)
    prev = jnp.where(row == 0, carry_ref[0:1, :], pltpu.roll(p_rw, 1, 0))
    carry_ref[0:1, :] = p_rw[tm - 1:tm, :]
    p_rw = p_rw + (prev - p_rw) * mu_ref[...]

    p_r = p_rw[:, 0:RWKV_WIDTH]
    p_k = p_rw[:, RWKV_WIDTH:2 * RWKV_WIDTH]
    p_v = p_rw[:, 2 * RWKV_WIDTH:3 * RWKV_WIDTH]
    p_lora = p_rw[:, 3 * RWKV_WIDTH:SHIFT_COLS]

    hsum = hsum_ref[...]
    z = w0_ref[...] + _dot(jnp.tanh(p_lora).astype(BF16), wlu_ref[...])
    ld_ref[...] = (-jnp.exp(-0.5)) * jax.nn.sigmoid(z)
    a = jax.nn.sigmoid(a0_ref[...] + _dot(p_lora.astype(BF16), alu_ref[...]))
    g_ref[...] = _dot(jax.nn.sigmoid(p_lora).astype(BF16), glu_ref[...])

    kk = p_k * kk_ref[...]
    ss = _split_dot(kk * kk, hsum)
    kk = kk / jnp.maximum(jnp.sqrt(ss), L2_EPS)
    k = p_k * (1.0 + (a - 1.0) * ka_ref[...])
    r_ref[...] = p_r
    k_ref[...] = k
    v_ref[...] = p_v
    kkn_ref[...] = kk
    b_ref[...] = kk * a
    bonus_ref[...] = _split_dot(p_r * k * rk_ref[...], hsum) * p_v

    pool_ref[POOL_HALO:POOL_HALO + tm, :] = p[:, SHIFT_COLS:]
    pos = t * tm + lax.broadcasted_iota(jnp.int32, (tm, POOL_GROUP), 0) + 1
    for gi, win in enumerate(POOL_WINDOWS):
        cols = slice(gi * POOL_GROUP, (gi + 1) * POOL_GROUP)
        cur = pool_ref[POOL_HALO:POOL_HALO + tm, cols]
        wsum = cur
        for j in range(1, win):
            wsum = wsum + pool_ref[POOL_HALO - j:POOL_HALO - j + tm, cols]
        count = jnp.minimum(pos, win).astype(F32)
        pooled = wsum / count - cur
        mixed = _dot(pooled.astype(BF16), wpool_ref[gi])
        ypool_ref[:, cols] = mixed * pscale_ref[:, cols]
    pool_ref[0:POOL_HALO, :] = pool_ref[tm:tm + POOL_HALO, :]


def _block_diag(x, bmask):
    return jnp.where(bmask, jnp.concatenate([x] * HEADS_PER_GROUP, axis=0), 0.0)


def _fold_rows(x):
    out = x[0:CHUNK]
    for hh in range(1, HEADS_PER_GROUP):
        out = out + x[hh * CHUNK:(hh + 1) * CHUNK]
    return out


def _chunk_kernel(r_ref, k_ref, v_ref, kk_ref, b_ref, ld_ref, q_ref, y0_ref, m_ref, n_ref):
    tb = r_ref.shape[0]
    gl = GROUP_LANES

    ri = lax.broadcasted_iota(jnp.int32, (tb, tb), 0)
    ci = lax.broadcasted_iota(jnp.int32, (tb, tb), 1)
    tri = jnp.where((ri // CHUNK == ci // CHUNK) & (ci <= ri), 1.0, 0.0).astype(BF16)
    ld = ld_ref[...]
    hi = ld.astype(BF16)
    lo = (ld - hi.astype(F32)).astype(BF16)
    cum = _dot(tri, hi) + _dot(tri, lo)

    bi = lax.broadcasted_iota(jnp.int32, (gl, gl), 0)
    bj = lax.broadcasted_iota(jnp.int32, (gl, gl), 1)
    bmask = (bi // HEAD_SIZE) == (bj // HEAD_SIZE)
    eye = bi == bj
    trow = lax.broadcasted_iota(jnp.int32, (CHUNK, gl), 0)
    scol = lax.broadcasted_iota(jnp.int32, (CHUNK, gl), 1) % CHUNK
    strict = scol < trow
    incl = scol <= trow
    eye_cat = jnp.where(scol == trow, 1.0, 0.0)
    bd = functools.partial(_block_diag, bmask=bmask)

    for c in range(tb // CHUNK):
        sl = slice(c * CHUNK, (c + 1) * CHUNK)
        lc = cum[sl]
        ldc = ld[sl]
        l_end = lc[CHUNK - 1:CHUNK, :]
        p_in = jnp.exp(lc)
        p_inv = jnp.exp(-lc)
        p_prev = jnp.exp(lc - ldc)
        p_end = jnp.exp(l_end - lc)
        r = r_ref[sl, :]
        k = k_ref[sl, :]
        v = v_ref[sl, :]
        kk = kk_ref[sl, :]
        b = b_ref[sl, :]
        rh = r * p_in
        ah = -kk * p_prev
        bh = b * p_inv
        kh = k * p_inv
        b_end = (b * p_end).astype(BF16)
        k_end = (k * p_end).astype(BF16)
        v_bd = bd(v).astype(BF16)

        lhs = jnp.concatenate([ah, rh], axis=0).astype(BF16)
        gb = _dot_nt(lhs, bd(bh).astype(BF16))
        gk = _dot_nt(lhs, bd(kh).astype(BF16))
        a_ab = jnp.where(strict, gb[:CHUNK], 0.0)
        a_rb = jnp.where(incl, gb[CHUNK:], 0.0).astype(BF16)
        a_ak = jnp.where(strict, gk[:CHUNK], 0.0).astype(BF16)
        a_rk = jnp.where(incl, gk[CHUNK:], 0.0).astype(BF16)

        tinv = eye_cat + a_ab
        pw = _dot(a_ab.astype(BF16), bd(a_ab).astype(BF16))
        nsq = CHUNK.bit_length() - 1
        for _ in range(nsq - 2):
            both = _dot(jnp.concatenate([pw, tinv], axis=0).astype(BF16), bd(pw).astype(BF16))
            tinv = tinv + both[CHUNK:]
            pw = both[:CHUNK]
        tinv = tinv + _dot(tinv.astype(BF16), bd(pw).astype(BF16))
        tb16 = tinv.astype(BF16)

        w = _dot(tb16, bd(ah).astype(BF16))
        zz = _dot(a_ak, v_bd)
        ut = _dot(tb16, bd(zz).astype(BF16))
        q_ref[sl, :] = rh + _dot(a_rb, bd(w).astype(BF16))
        y0_ref[sl, :] = _dot(a_rb, bd(ut).astype(BF16)) + _dot(a_rk, v_bd)

        wu = jnp.concatenate([w, ut], axis=1).astype(BF16)
        mn = _dot_tn(b_end, wu)
        kv = _dot_tn(k_end, v.astype(BF16))
        m_bd = jnp.where(eye, jnp.exp(l_end), 0.0) + jnp.where(bmask, mn[:, :gl], 0.0)
        n_bd = jnp.where(bmask, mn[:, gl:] + kv, 0.0)
        m_ref[c] = _fold_rows(m_bd)
        n_ref[c] = _fold_rows(n_bd)


def _scan_kernel(q_ref, y0_ref, m_ref, n_ref, g_ref, bonus_ref, lnw_ref, lnb_ref, hmean_ref,
                 y_ref, h_ref):
    gl = GROUP_LANES
    tb = q_ref.shape[0]

    @pl.when(pl.program_id(2) == 0)
    def _():
        h_ref[...] = jnp.zeros_like(h_ref)

    bi = lax.broadcasted_iota(jnp.int32, (gl, gl), 0)
    bj = lax.broadcasted_iota(jnp.int32, (gl, gl), 1)
    bmask = (bi // HEAD_SIZE) == (bj // HEAD_SIZE)
    hmean = hmean_ref[...]

    for c in range(tb // CHUNK):
        sl = slice(c * CHUNK, (c + 1) * CHUNK)
        hb = h_ref[...].astype(BF16)
        y = _dot(q_ref[sl, :].astype(BF16), hb) + y0_ref[sl, :]
        m_bd = _block_diag(m_ref[c], bmask).astype(BF16)
        h_ref[...] = _dot(m_bd, hb) + _block_diag(n_ref[c], bmask)

        mean = _split_dot(y, hmean)
        d = y - mean
        var = _split_dot(d * d, hmean)
        yn = d * lax.rsqrt(var + GN_EPS) * lnw_ref[...] + lnb_ref[...]
        y_ref[sl, :] = (yn + bonus_ref[sl, :]) * g_ref[sl, :]


def _post_kernel(x1_ref, yrw_ref, ypool_ref, wout_ref, n2_ref, wg_ref, wu_ref, wd_ref, nf_ref,
                 o_ref):
    mix = _dot(yrw_ref[...].astype(BF16), wout_ref[0:RWKV_WIDTH, :])
    mix = mix + _dot(ypool_ref[...].astype(BF16), wout_ref[RWKV_WIDTH:, :])
    x2 = x1_ref[...] + mix
    h = _rms(x2, n2_ref[...]).astype(BF16)
    x3 = x2 + 0.5 * _swiglu(h, wg_ref, wu_ref, wd_ref)
    o_ref[...] = _rms(x3, nf_ref[...])


def _const_spec(shape):
    nd = len(shape)
    return pl.BlockSpec(shape, lambda *_: (0,) * nd, pipeline_mode=pl.Buffered(1))


def _head_block_ones(n, scale):
    i = jnp.arange(n) // HEAD_SIZE
    return jnp.where(i[:, None] == i[None, :], scale, 0.0).astype(BF16)


def kernel(x, ffn1_norm, ffn1_w_gate, ffn1_w_up, ffn1_w_down, mix_norm, w_in, mu_shift, w0, w_lora_up, a0, a_lora_up, g_lora_up, k_k, k_a, r_k, ln_w, ln_b, w_pool, pool_scale, w_out, ffn2_norm, ffn2_w_gate, ffn2_w_up, ffn2_w_down, final_norm):
    bsz, seq, _ = x.shape
    assert ffn1_norm.shape[0] == 1, "one trunk layer"
    tm = min(TOKEN_TILE, seq)
    tb = min(SCAN_TILE, seq)
    assert seq % tm == 0 and seq % tb == 0 and tb % CHUNK == 0 and tm >= POOL_HALO
    n_chunks = seq // CHUNK
    cpt = tb // CHUNK

    hsum = _head_block_ones(RWKV_WIDTH, 1.0)
    hmean = _head_block_ones(GROUP_LANES, 1.0 / HEAD_SIZE)
    row = lambda v: v.reshape(1, -1).astype(F32)

    tok = lambda width: pl.BlockSpec((None, tm, width), lambda b, t: (b, t, 0))
    tok_shape = lambda width: jax.ShapeDtypeStruct((bsz, seq, width), F32)

    rkv = 3 * RWKV_WIDTH
    w_in_p = jnp.concatenate(
        [w_in[0][:, :RWKV_COLS], jnp.zeros((D_MODEL, LORA_PAD - LORA_COLS), F32),
         w_in[0][:, RWKV_COLS:]], axis=1)
    mu_p = jnp.concatenate([mu_shift[0], jnp.zeros((LORA_PAD - LORA_COLS,), F32)])

    def lora_rows(w_up, start):
        pad = jnp.zeros((LORA_PAD, RWKV_WIDTH), F32)
        return pad.at[start:start + w_up.shape[0]].set(w_up).astype(BF16)

    l = 0
    if True:
        pre_in = [
            (x, tok(D_MODEL)),
            (row(ffn1_norm[l]), None),
            (ffn1_w_gate[l].astype(BF16), None),
            (ffn1_w_up[l].astype(BF16), None),
            (ffn1_w_down[l].astype(BF16), None),
            (row(mix_norm[l]), None),
            (w_in_p.astype(BF16), None),
            (row(mu_p), None),
            (row(w0[l]), None),
            (lora_rows(w_lora_up[l], 0), None),
            (row(a0[l]), None),
            (lora_rows(a_lora_up[l], D_DECAY_LORA), None),
            (lora_rows(g_lora_up[l], D_DECAY_LORA + D_AAA_LORA), None),
            (row(k_k[l]), None),
            (row(k_a[l]), None),
            (row(r_k[l]), None),
            (w_pool[l].astype(BF16), None),
            (row(pool_scale[l]), None),
            (hsum, None),
        ]
        pre_args = [a for a, _ in pre_in]
        pre_specs = [s if s is not None else _const_spec(a.shape) for a, s in pre_in]
        n_rw_out = 9
        x1, r, k, v, kk, b, ld, g, bonus, ypool = pl.pallas_call(
            _pre_kernel,
            grid=(bsz, seq // tm),
            in_specs=pre_specs,
            out_specs=[tok(D_MODEL)] + [tok(RWKV_WIDTH)] * n_rw_out,
            out_shape=[tok_shape(D_MODEL)] + [tok_shape(RWKV_WIDTH)] * n_rw_out,
            scratch_shapes=[pltpu.VMEM((8, SHIFT_COLS), F32),
                            pltpu.VMEM((POOL_HALO + tm, POOL_WIDTH), F32)],
            compiler_params=pltpu.CompilerParams(
                dimension_semantics=("arbitrary", "arbitrary"), vmem_limit_bytes=VMEM_LIMIT),
            name="pre",
        )(*pre_args)

        grp = pl.BlockSpec((None, tb, GROUP_LANES), lambda bb, gg, tt: (bb, tt, gg))
        mat = pl.BlockSpec((None, cpt, CHUNK, GROUP_LANES), lambda bb, gg, tt: (bb, tt, 0, gg))
        mat_shape = jax.ShapeDtypeStruct((bsz, n_chunks, CHUNK, RWKV_WIDTH), F32)
        q, y0, m, n = pl.pallas_call(
            _chunk_kernel,
            grid=(bsz, N_GROUPS, seq // tb),
            in_specs=[grp] * 6,
            out_specs=[grp, grp, mat, mat],
            out_shape=[tok_shape(RWKV_WIDTH), tok_shape(RWKV_WIDTH), mat_shape, mat_shape],
            compiler_params=pltpu.CompilerParams(
                dimension_semantics=("arbitrary", "arbitrary", "arbitrary"),
                vmem_limit_bytes=VMEM_LIMIT),
            name="chunk",
        )(r, k, v, kk, b, ld)

        vec = pl.BlockSpec((1, GROUP_LANES), lambda bb, gg, tt: (0, gg))
        y_rw = pl.pallas_call(
            _scan_kernel,
            grid=(bsz, N_GROUPS, seq // tb),
            in_specs=[grp, grp, mat, mat, grp, grp, vec, vec,
                      pl.BlockSpec((GROUP_LANES, GROUP_LANES), lambda bb, gg, tt: (0, 0))],
            out_specs=grp,
            out_shape=tok_shape(RWKV_WIDTH),
            scratch_shapes=[pltpu.VMEM((GROUP_LANES, GROUP_LANES), F32)],
            compiler_params=pltpu.CompilerParams(
                dimension_semantics=("arbitrary", "arbitrary", "arbitrary"),
                vmem_limit_bytes=VMEM_LIMIT),
            name="scan",
        )(q, y0, m, n, g, bonus, row(ln_w[l]), row(ln_b[l]), hmean)

        post_in = [
            (x1, tok(D_MODEL)),
            (y_rw, tok(RWKV_WIDTH)),
            (ypool, tok(POOL_WIDTH)),
            (w_out[l].astype(BF16), None),
            (row(ffn2_norm[l]), None),
            (ffn2_w_gate[l].astype(BF16), None),
            (ffn2_w_up[l].astype(BF16), None),
            (ffn2_w_down[l].astype(BF16), None),
            (row(final_norm), None),
        ]
        post_args = [a for a, _ in post_in]
        post_specs = [s if s is not None else _const_spec(a.shape) for a, s in post_in]
        x = pl.pallas_call(
            functools.partial(_post_kernel),
            grid=(bsz, seq // tm),
            in_specs=post_specs,
            out_specs=tok(D_MODEL),
            out_shape=tok_shape(D_MODEL),
            compiler_params=pltpu.CompilerParams(
                dimension_semantics=("arbitrary", "arbitrary"), vmem_limit_bytes=VMEM_LIMIT),
            name="post",
        )(*post_args)
    return x
```

```python
import functools

import jax
import jax.numpy as jnp
from jax import lax
from jax.experimental import pallas as pl
from jax.experimental.pallas import tpu as pltpu

F32 = jnp.float32
BF16 = jnp.bfloat16

D_MODEL = 1024
RWKV_WIDTH = 512
POOL_WIDTH = 512
HEAD_SIZE = 64
D_DECAY_LORA = 32
D_AAA_LORA = 32
D_GATE_LORA = 96
LORA_COLS = D_DECAY_LORA + D_AAA_LORA + D_GATE_LORA
LORA_PAD = 256
POOL_WINDOWS = (2, 4, 8, 16)
POOL_GROUP = 128
POOL_HALO = 16
D_FF = 2816
RWKV_COLS = 3 * RWKV_WIDTH + LORA_COLS
SHIFT_COLS = 3 * RWKV_WIDTH + LORA_PAD
RMS_EPS = 1e-6
GN_EPS = 64e-5
L2_EPS = 1e-12

CHUNK = 64
GROUP_LANES = 256
HEADS_PER_GROUP = GROUP_LANES // HEAD_SIZE
N_GROUPS = RWKV_WIDTH // GROUP_LANES
FF_CHUNK = 1408

TOKEN_TILE = 256
SCAN_TILE = 256
VMEM_LIMIT = 56 * 1024 * 1024


def _dot(a, b):
    return jnp.dot(a, b, preferred_element_type=F32)


def _dot_nt(a, b):
    return lax.dot_general(a, b, (((1,), (1,)), ((), ())), preferred_element_type=F32)


def _dot_tn(a, b):
    return lax.dot_general(a, b, (((0,), (0,)), ((), ())), preferred_element_type=F32)


def _split_dot(x, w):
    hi = x.astype(BF16)
    lo = (x - hi.astype(F32)).astype(BF16)
    return _dot(hi, w) + _dot(lo, w)


def _rms(x, g):
    return x * lax.rsqrt(jnp.mean(x * x, axis=-1, keepdims=True) + RMS_EPS) * g


def _swiglu(hb, wg_ref, wu_ref, wd_ref):
    acc = None
    for c0 in range(0, D_FF, FF_CHUNK):
        gate = _dot(hb, wg_ref[:, c0:c0 + FF_CHUNK])
        up = _dot(hb, wu_ref[:, c0:c0 + FF_CHUNK])
        act = (gate * jax.nn.sigmoid(gate) * up).astype(BF16)
        part = _dot(act, wd_ref[c0:c0 + FF_CHUNK, :])
        acc = part if acc is None else acc + part
    return acc


def _pre_kernel(x_ref, n1_ref, wg_ref, wu_ref, wd_ref, nm_ref, win_ref, mu_ref, w0_ref,
                wlu_ref, a0_ref, alu_ref, glu_ref, kk_ref, ka_ref, rk_ref, wpool_ref,
                pscale_ref, hsum_ref,
                x1_ref, r_ref, k_ref, v_ref, kkn_ref, b_ref, ld_ref, g_ref, bonus_ref,
                ypool_ref, carry_ref, pool_ref):
    tm = x_ref.shape[0]
    t = pl.program_id(1)

    @pl.when(t == 0)
    def _():
        carry_ref[...] = jnp.zeros_like(carry_ref)
        pool_ref[0:POOL_HALO, :] = jnp.zeros((POOL_HALO, POOL_WIDTH), F32)

    x = x_ref[...]
    h = _rms(x, n1_ref[...]).astype(BF16)
    x1 = x + 0.5 * _swiglu(h, wg_ref, wu_ref, wd_ref)
    x1_ref[...] = x1

    h2 = _rms(x1, nm_ref[...]).astype(BF16)
    p = _dot(h2, win_ref[...])

    p_rw = p[:, :SHIFT_COLS]
    row = lax.broadcasted_iota(jnp.int32, (tm, SHIFT_COLS), 0)
    prev = jnp.where(row == 0, carry_ref[0:1, :], pltpu.roll(p_rw, 1, 0))
    carry_ref[0:1, :] = p_rw[tm - 1:tm, :]
    p_rw = p_rw + (prev - p_rw) * mu_ref[...]

    p_r = p_rw[:, 0:RWKV_WIDTH]
    p_k = p_rw[:, RWKV_WIDTH:2 * RWKV_WIDTH]
    p_v = p_rw[:, 2 * RWKV_WIDTH:3 * RWKV_WIDTH]
    p_lora = p_rw[:, 3 * RWKV_WIDTH:SHIFT_COLS]

    hsum = hsum_ref[...]
    z = w0_ref[...] + _dot(jnp.tanh(p_lora).astype(BF16), wlu_ref[...])
    ld_ref[...] = (-jnp.exp(-0.5)) * jax.nn.sigmoid(z)
    a = jax.nn.sigmoid(a0_ref[...] + _dot(p_lora.astype(BF16), alu_ref[...]))
    g_ref[...] = _dot(jax.nn.sigmoid(p_lora).astype(BF16), glu_ref[...])

    kk = p_k * kk_ref[...]
    ss = _split_dot(kk * kk, hsum)
    kk = kk / jnp.maximum(jnp.sqrt(ss), L2_EPS)
    k = p_k * (1.0 + (a - 1.0) * ka_ref[...])
    r_ref[...] = p_r
    k_ref[...] = k
    v_ref[...] = p_v
    kkn_ref[...] = kk
    b_ref[...] = kk * a
    bonus_ref[...] = _split_dot(p_r * k * rk_ref[...], hsum) * p_v

    pool_ref[POOL_HALO:POOL_HALO + tm, :] = p[:, SHIFT_COLS:]
    pos = t * tm + lax.broadcasted_iota(jnp.int32, (tm, POOL_GROUP), 0) + 1
    for gi, win in enumerate(POOL_WINDOWS):
        cols = slice(gi * POOL_GROUP, (gi + 1) * POOL_GROUP)
        cur = pool_ref[POOL_HALO:POOL_HALO + tm, cols]
        wsum = cur
        for j in range(1, win):
            wsum = wsum + pool_ref[POOL_HALO - j:POOL_HALO - j + tm, cols]
        count = jnp.minimum(pos, win).astype(F32)
        pooled = wsum / count - cur
        mixed = _dot(pooled.astype(BF16), wpool_ref[gi])
        ypool_ref[:, cols] = mixed * pscale_ref[:, cols]
    pool_ref[0:POOL_HALO, :] = pool_ref[tm:tm + POOL_HALO, :]


def _block_diag(x, bmask):
    return jnp.where(bmask, jnp.concatenate([x] * HEADS_PER_GROUP, axis=0), 0.0)


def _fold_rows(x):
    out = x[0:CHUNK]
    for hh in range(1, HEADS_PER_GROUP):
        out = out + x[hh * CHUNK:(hh + 1) * CHUNK]
    return out


def _chunk_kernel(r_ref, k_ref, v_ref, kk_ref, b_ref, ld_ref, q_ref, y0_ref, m_ref, n_ref):
    tb = r_ref.shape[0]
    gl = GROUP_LANES

    ri = lax.broadcasted_iota(jnp.int32, (tb, tb), 0)
    ci = lax.broadcasted_iota(jnp.int32, (tb, tb), 1)
    tri = jnp.where((ri // CHUNK == ci // CHUNK) & (ci <= ri), 1.0, 0.0).astype(BF16)
    ld = ld_ref[...]
    hi = ld.astype(BF16)
    lo = (ld - hi.astype(F32)).astype(BF16)
    cum = _dot(tri, hi) + _dot(tri, lo)

    bi = lax.broadcasted_iota(jnp.int32, (gl, gl), 0)
    bj = lax.broadcasted_iota(jnp.int32, (gl, gl), 1)
    bmask = (bi // HEAD_SIZE) == (bj // HEAD_SIZE)
    eye = bi == bj
    trow = lax.broadcasted_iota(jnp.int32, (CHUNK, gl), 0)
    scol = lax.broadcasted_iota(jnp.int32, (CHUNK, gl), 1) % CHUNK
    strict = scol < trow
    incl = scol <= trow
    eye_cat = jnp.where(scol == trow, 1.0, 0.0)
    bd = functools.partial(_block_diag, bmask=bmask)

    chunks = range(tb // CHUNK)
    sls = [slice(c * CHUNK, (c + 1) * CHUNK) for c in chunks]
    bdb = lambda x: bd(x).astype(BF16)

    lc = [cum[sl] for sl in sls]
    l_end = [x[CHUNK - 1:CHUNK, :] for x in lc]
    v = [v_ref[sl, :] for sl in sls]
    rh = [r_ref[sl, :] * jnp.exp(lc[c]) for c, sl in enumerate(sls)]
    ah = [-kk_ref[sl, :] * jnp.exp(lc[c] - ld[sl]) for c, sl in enumerate(sls)]
    p_inv = [jnp.exp(-x) for x in lc]
    p_end = [jnp.exp(l_end[c] - lc[c]) for c in chunks]
    bh = [b_ref[sl, :] * p_inv[c] for c, sl in enumerate(sls)]
    kh = [k_ref[sl, :] * p_inv[c] for c, sl in enumerate(sls)]
    b_end = [(b_ref[sl, :] * p_end[c]).astype(BF16) for c, sl in enumerate(sls)]
    k_end = [(k_ref[sl, :] * p_end[c]).astype(BF16) for c, sl in enumerate(sls)]
    v_bd = [bdb(x) for x in v]

    lhs = [jnp.concatenate([ah[c], rh[c]], axis=0).astype(BF16) for c in chunks]
    gb = [_dot_nt(lhs[c], bdb(bh[c])) for c in chunks]
    gk = [_dot_nt(lhs[c], bdb(kh[c])) for c in chunks]
    a_ab = [jnp.where(strict, x[:CHUNK], 0.0) for x in gb]
    a_rb = [jnp.where(incl, x[CHUNK:], 0.0).astype(BF16) for x in gb]
    a_ak = [jnp.where(strict, x[:CHUNK], 0.0).astype(BF16) for x in gk]
    a_rk = [jnp.where(incl, x[CHUNK:], 0.0).astype(BF16) for x in gk]

    tinv = [eye_cat + x for x in a_ab]
    pw = [_dot(x.astype(BF16), bdb(x)) for x in a_ab]
    for _ in range(CHUNK.bit_length() - 3):
        both = [_dot(jnp.concatenate([pw[c], tinv[c]], axis=0).astype(BF16), bdb(pw[c]))
                for c in chunks]
        tinv = [tinv[c] + both[c][CHUNK:] for c in chunks]
        pw = [x[:CHUNK] for x in both]
    tinv = [tinv[c] + _dot(tinv[c].astype(BF16), bdb(pw[c])) for c in chunks]
    tb16 = [x.astype(BF16) for x in tinv]

    w = [_dot(tb16[c], bdb(ah[c])) for c in chunks]
    zz = [_dot(a_ak[c], v_bd[c]) for c in chunks]
    ut = [_dot(tb16[c], bdb(zz[c])) for c in chunks]
    for c, sl in enumerate(sls):
        q_ref[sl, :] = rh[c] + _dot(a_rb[c], bdb(w[c]))
        y0_ref[sl, :] = _dot(a_rb[c], bdb(ut[c])) + _dot(a_rk[c], v_bd[c])

    for c in chunks:
        wu = jnp.concatenate([w[c], ut[c]], axis=1).astype(BF16)
        mn = _dot_tn(b_end[c], wu)
        kv = _dot_tn(k_end[c], v[c].astype(BF16))
        m_bd = jnp.where(eye, jnp.exp(l_end[c]), 0.0) + jnp.where(bmask, mn[:, :gl], 0.0)
        n_bd = jnp.where(bmask, mn[:, gl:] + kv, 0.0)
        m_ref[c] = _fold_rows(m_bd)
        n_ref[c] = _fold_rows(n_bd)


def _scan_kernel(q_ref, y0_ref, m_ref, n_ref, g_ref, bonus_ref, lnw_ref, lnb_ref, hmean_ref,
                 y_ref, h_ref):
    gl = GROUP_LANES
    tb = q_ref.shape[0]

    @pl.when(pl.program_id(2) == 0)
    def _():
        h_ref[...] = jnp.zeros_like(h_ref)

    bi = lax.broadcasted_iota(jnp.int32, (gl, gl), 0)
    bj = lax.broadcasted_iota(jnp.int32, (gl, gl), 1)
    bmask = (bi // HEAD_SIZE) == (bj // HEAD_SIZE)
    hmean = hmean_ref[...]

    for c in range(tb // CHUNK):
        sl = slice(c * CHUNK, (c + 1) * CHUNK)
        hb = h_ref[...].astype(BF16)
        y = _dot(q_ref[sl, :].astype(BF16), hb) + y0_ref[sl, :]
        m_bd = _block_diag(m_ref[c], bmask).astype(BF16)
        h_ref[...] = _dot(m_bd, hb) + _block_diag(n_ref[c], bmask)

        mean = _split_dot(y, hmean)
        d = y - mean
        var = _split_dot(d * d, hmean)
        yn = d * lax.rsqrt(var + GN_EPS) * lnw_ref[...] + lnb_ref[...]
        y_ref[sl, :] = (yn + bonus_ref[sl, :]) * g_ref[sl, :]


def _post_kernel(x1_ref, yrw_ref, ypool_ref, wout_ref, n2_ref, wg_ref, wu_ref, wd_ref, nf_ref,
                 o_ref):
    mix = _dot(yrw_ref[...].astype(BF16), wout_ref[0:RWKV_WIDTH, :])
    mix = mix + _dot(ypool_ref[...].astype(BF16), wout_ref[RWKV_WIDTH:, :])
    x2 = x1_ref[...] + mix
    h = _rms(x2, n2_ref[...]).astype(BF16)
    x3 = x2 + 0.5 * _swiglu(h, wg_ref, wu_ref, wd_ref)
    o_ref[...] = _rms(x3, nf_ref[...])


def _const_spec(shape):
    nd = len(shape)
    return pl.BlockSpec(shape, lambda *_: (0,) * nd, pipeline_mode=pl.Buffered(1))


def _head_block_ones(n, scale):
    i = jnp.arange(n) // HEAD_SIZE
    return jnp.where(i[:, None] == i[None, :], scale, 0.0).astype(BF16)


def kernel(x, ffn1_norm, ffn1_w_gate, ffn1_w_up, ffn1_w_down, mix_norm, w_in, mu_shift, w0, w_lora_up, a0, a_lora_up, g_lora_up, k_k, k_a, r_k, ln_w, ln_b, w_pool, pool_scale, w_out, ffn2_norm, ffn2_w_gate, ffn2_w_up, ffn2_w_down, final_norm):
    bsz, seq, _ = x.shape
    assert ffn1_norm.shape[0] == 1, "one trunk layer"
    tm = min(TOKEN_TILE, seq)
    tb = min(SCAN_TILE, seq)
    assert seq % tm == 0 and seq % tb == 0 and tb % CHUNK == 0 and tm >= POOL_HALO
    n_chunks = seq // CHUNK
    cpt = tb // CHUNK

    hsum = _head_block_ones(RWKV_WIDTH, 1.0)
    hmean = _head_block_ones(GROUP_LANES, 1.0 / HEAD_SIZE)
    row = lambda v: v.reshape(1, -1).astype(F32)

    tok = lambda width: pl.BlockSpec((None, tm, width), lambda b, t: (b, t, 0))
    tok_shape = lambda width: jax.ShapeDtypeStruct((bsz, seq, width), F32)

    rkv = 3 * RWKV_WIDTH
    w_in_p = jnp.concatenate(
        [w_in[0][:, :RWKV_COLS], jnp.zeros((D_MODEL, LORA_PAD - LORA_COLS), F32),
         w_in[0][:, RWKV_COLS:]], axis=1)
    mu_p = jnp.concatenate([mu_shift[0], jnp.zeros((LORA_PAD - LORA_COLS,), F32)])

    def lora_rows(w_up, start):
        pad = jnp.zeros((LORA_PAD, RWKV_WIDTH), F32)
        return pad.at[start:start + w_up.shape[0]].set(w_up).astype(BF16)

    l = 0
    if True:
        pre_in = [
            (x, tok(D_MODEL)),
            (row(ffn1_norm[l]), None),
            (ffn1_w_gate[l].astype(BF16), None),
            (ffn1_w_up[l].astype(BF16), None),
            (ffn1_w_down[l].astype(BF16), None),
            (row(mix_norm[l]), None),
            (w_in_p.astype(BF16), None),
            (row(mu_p), None),
            (row(w0[l]), None),
            (lora_rows(w_lora_up[l], 0), None),
            (row(a0[l]), None),
            (lora_rows(a_lora_up[l], D_DECAY_LORA), None),
            (lora_rows(g_lora_up[l], D_DECAY_LORA + D_AAA_LORA), None),
            (row(k_k[l]), None),
            (row(k_a[l]), None),
            (row(r_k[l]), None),
            (w_pool[l].astype(BF16), None),
            (row(pool_scale[l]), None),
            (hsum, None),
        ]
        pre_args = [a for a, _ in pre_in]
        pre_specs = [s if s is not None else _const_spec(a.shape) for a, s in pre_in]
        n_rw_out = 9
        x1, r, k, v, kk, b, ld, g, bonus, ypool = pl.pallas_call(
            _pre_kernel,
            grid=(bsz, seq // tm),
            in_specs=pre_specs,
            out_specs=[tok(D_MODEL)] + [tok(RWKV_WIDTH)] * n_rw_out,
            out_shape=[tok_shape(D_MODEL)] + [tok_shape(RWKV_WIDTH)] * n_rw_out,
            scratch_shapes=[pltpu.VMEM((8, SHIFT_COLS), F32),
                            pltpu.VMEM((POOL_HALO + tm, POOL_WIDTH), F32)],
            compiler_params=pltpu.CompilerParams(
                dimension_semantics=("arbitrary", "arbitrary"), vmem_limit_bytes=VMEM_LIMIT),
            name="pre",
        )(*pre_args)

        grp = pl.BlockSpec((None, tb, GROUP_LANES), lambda bb, gg, tt: (bb, tt, gg))
        mat = pl.BlockSpec((None, cpt, CHUNK, GROUP_LANES), lambda bb, gg, tt: (bb, tt, 0, gg))
        mat_shape = jax.ShapeDtypeStruct((bsz, n_chunks, CHUNK, RWKV_WIDTH), F32)
        q, y0, m, n = pl.pallas_call(
            _chunk_kernel,
            grid=(bsz, N_GROUPS, seq // tb),
            in_specs=[grp] * 6,
            out_specs=[grp, grp, mat, mat],
            out_shape=[tok_shape(RWKV_WIDTH), tok_shape(RWKV_WIDTH), mat_shape, mat_shape],
            compiler_params=pltpu.CompilerParams(
                dimension_semantics=("arbitrary", "arbitrary", "arbitrary"),
                vmem_limit_bytes=VMEM_LIMIT),
            name="chunk",
        )(r, k, v, kk, b, ld)

        vec = pl.BlockSpec((1, GROUP_LANES), lambda bb, gg, tt: (0, gg))
        y_rw = pl.pallas_call(
            _scan_kernel,
            grid=(bsz, N_GROUPS, seq // tb),
            in_specs=[grp, grp, mat, mat, grp, grp, vec, vec,
                      pl.BlockSpec((GROUP_LANES, GROUP_LANES), lambda bb, gg, tt: (0, 0))],
            out_specs=grp,
            out_shape=tok_shape(RWKV_WIDTH),
            scratch_shapes=[pltpu.VMEM((GROUP_LANES, GROUP_LANES), F32)],
            compiler_params=pltpu.CompilerParams(
                dimension_semantics=("arbitrary", "arbitrary", "arbitrary"),
                vmem_limit_bytes=VMEM_LIMIT),
            name="scan",
        )(q, y0, m, n, g, bonus, row(ln_w[l]), row(ln_b[l]), hmean)

        post_in = [
            (x1, tok(D_MODEL)),
            (y_rw, tok(RWKV_WIDTH)),
            (ypool, tok(POOL_WIDTH)),
            (w_out[l].astype(BF16), None),
            (row(ffn2_norm[l]), None),
            (ffn2_w_gate[l].astype(BF16), None),
            (ffn2_w_up[l].astype(BF16), None),
            (ffn2_w_down[l].astype(BF16), None),
            (row(final_norm), None),
        ]
        post_args = [a for a, _ in post_in]
        post_specs = [s if s is not None else _const_spec(a.shape) for a, s in post_in]
        x = pl.pallas_call(
            functools.partial(_post_kernel),
            grid=(bsz, seq // tm),
            in_specs=post_specs,
            out_specs=tok(D_MODEL),
            out_shape=tok_shape(D_MODEL),
            compiler_params=pltpu.CompilerParams(
                dimension_semantics=("arbitrary", "arbitrary"), vmem_limit_bytes=VMEM_LIMIT),
            name="post",
        )(*post_args)
    return x
```

```python
import functools

import jax
import jax.numpy as jnp
from jax import lax
from jax.experimental import pallas as pl
from jax.experimental.pallas import tpu as pltpu

F32 = jnp.float32
BF16 = jnp.bfloat16

D_MODEL = 1024
RWKV_WIDTH = 512
POOL_WIDTH = 512
HEAD_SIZE = 64
D_DECAY_LORA = 32
D_AAA_LORA = 32
D_GATE_LORA = 96
LORA_COLS = D_DECAY_LORA + D_AAA_LORA + D_GATE_LORA
LORA_PAD = 256
POOL_WINDOWS = (2, 4, 8, 16)
POOL_GROUP = 128
POOL_HALO = 16
D_FF = 2816
RWKV_COLS = 3 * RWKV_WIDTH + LORA_COLS
SHIFT_COLS = 3 * RWKV_WIDTH + LORA_PAD
RMS_EPS = 1e-6
GN_EPS = 64e-5
L2_EPS = 1e-12

CHUNK = 64
GROUP_LANES = 256
HEADS_PER_GROUP = GROUP_LANES // HEAD_SIZE
N_GROUPS = RWKV_WIDTH // GROUP_LANES
FF_CHUNK = 1408

TOKEN_TILE = 512
SCAN_TILE = 512
SCAN_BATCH = 2
VMEM_LIMIT = 56 * 1024 * 1024


def _dot(a, b):
    return jnp.dot(a, b, preferred_element_type=F32)


def _dot_nt(a, b):
    return lax.dot_general(a, b, (((1,), (1,)), ((), ())), preferred_element_type=F32)


def _dot_tn(a, b):
    return lax.dot_general(a, b, (((0,), (0,)), ((), ())), preferred_element_type=F32)


def _split_dot(x, w):
    hi = x.astype(BF16)
    lo = (x - hi.astype(F32)).astype(BF16)
    return _dot(hi, w) + _dot(lo, w)


def _head_sums(x, ones_bd):
    return _dot(x.astype(BF16), ones_bd)


def _rms(x, g):
    return x * lax.rsqrt(jnp.mean(x * x, axis=-1, keepdims=True) + RMS_EPS) * g


def _swiglu(hb, wg_ref, wu_ref, wd_ref):
    acc = None
    for c0 in range(0, D_FF, FF_CHUNK):
        gate = _dot(hb, wg_ref[:, c0:c0 + FF_CHUNK])
        up = _dot(hb, wu_ref[:, c0:c0 + FF_CHUNK])
        act = (gate * jax.nn.sigmoid(gate) * up).astype(BF16)
        part = _dot(act, wd_ref[c0:c0 + FF_CHUNK, :])
        acc = part if acc is None else acc + part
    return acc


def _pre_kernel(x_ref, n1_ref, wg_ref, wu_ref, wd_ref, nm_ref, win_ref, mu_ref, w0_ref,
                wlu_ref, a0_ref, alu_ref, glu_ref, kk_ref, ka_ref, rk_ref, wpool_ref,
                pscale_ref, hsum_ref,
                x1_ref, r_ref, k_ref, v_ref, kkn_ref, b_ref, ld_ref, g_ref, bonus_ref,
                ypool_ref, carry_ref, pool_ref):
    tm = x_ref.shape[0]
    t = pl.program_id(1)

    @pl.when(t == 0)
    def _():
        carry_ref[...] = jnp.zeros_like(carry_ref)
        pool_ref[0:POOL_HALO, :] = jnp.zeros((POOL_HALO, POOL_WIDTH), F32)

    x = x_ref[...]
    h = _rms(x, n1_ref[...]).astype(BF16)
    x1 = x + 0.5 * _swiglu(h, wg_ref, wu_ref, wd_ref)
    x1_ref[...] = x1

    h2 = _rms(x1, nm_ref[...]).astype(BF16)
    p = _dot(h2, win_ref[...])

    p_rw = p[:, :SHIFT_COLS]
    row = lax.broadcasted_iota(jnp.int32, (tm, SHIFT_COLS), 0)
    prev = jnp.where(row == 0, carry_ref[0:1, :], pltpu.roll(p_rw, 1, 0))
    carry_ref[0:1, :] = p_rw[tm - 1:tm, :]
    p_rw = p_rw + (prev - p_rw) * mu_ref[...]

    p_r = p_rw[:, 0:RWKV_WIDTH]
    p_k = p_rw[:, RWKV_WIDTH:2 * RWKV_WIDTH]
    p_v = p_rw[:, 2 * RWKV_WIDTH:3 * RWKV_WIDTH]
    p_lora = p_rw[:, 3 * RWKV_WIDTH:SHIFT_COLS]

    hsum = hsum_ref[...]
    z = w0_ref[...] + _dot(jnp.tanh(p_lora).astype(BF16), wlu_ref[...])
    ld_ref[...] = (-jnp.exp(-0.5)) * jax.nn.sigmoid(z)
    a = jax.nn.sigmoid(a0_ref[...] + _dot(p_lora.astype(BF16), alu_ref[...]))
    g_ref[...] = _dot(jax.nn.sigmoid(p_lora).astype(BF16), glu_ref[...]).astype(g_ref.dtype)

    kk = p_k * kk_ref[...]
    k = p_k * (1.0 + (a - 1.0) * ka_ref[...])
    rkr = p_r * k * rk_ref[...]
    r_ref[...] = p_r.astype(r_ref.dtype)
    k_ref[...] = k.astype(k_ref.dtype)
    v_ref[...] = p_v.astype(v_ref.dtype)
    for gi in range(N_GROUPS):
        gs = slice(gi * GROUP_LANES, (gi + 1) * GROUP_LANES)
        kk_g = kk[:, gs]
        kk_g = kk_g / jnp.maximum(jnp.sqrt(_head_sums(kk_g * kk_g, hsum)), L2_EPS)
        kkn_ref[:, gs] = kk_g.astype(kkn_ref.dtype)
        b_ref[:, gs] = (kk_g * a[:, gs]).astype(b_ref.dtype)
        bonus_ref[:, gs] = (_head_sums(rkr[:, gs], hsum) * p_v[:, gs]).astype(bonus_ref.dtype)

    pool_ref[POOL_HALO:POOL_HALO + tm, :] = p[:, SHIFT_COLS:]
    pos = t * tm + lax.broadcasted_iota(jnp.int32, (tm, POOL_GROUP), 0) + 1
    for gi, win in enumerate(POOL_WINDOWS):
        cols = slice(gi * POOL_GROUP, (gi + 1) * POOL_GROUP)
        cur = pool_ref[POOL_HALO:POOL_HALO + tm, cols]
        wsum = cur
        for j in range(1, win):
            wsum = wsum + pool_ref[POOL_HALO - j:POOL_HALO - j + tm, cols]
        count = jnp.minimum(pos, win).astype(F32)
        pooled = wsum / count - cur
        mixed = _dot(pooled.astype(BF16), wpool_ref[gi])
        ypool_ref[:, cols] = (mixed * pscale_ref[:, cols]).astype(ypool_ref.dtype)
    pool_ref[0:POOL_HALO, :] = pool_ref[tm:tm + POOL_HALO, :]


def _block_diag(x, bmask):
    return jnp.where(bmask, jnp.concatenate([x] * HEADS_PER_GROUP, axis=0), 0.0)


def _fold_rows(x):
    out = x[0:CHUNK]
    for hh in range(1, HEADS_PER_GROUP):
        out = out + x[hh * CHUNK:(hh + 1) * CHUNK]
    return out


def _chunk_kernel(r_ref, k_ref, v_ref, kk_ref, b_ref, ld_ref, q_ref, y0_ref, mt_ref, nt_ref):
    tb = r_ref.shape[0]
    gl = GROUP_LANES

    ri = lax.broadcasted_iota(jnp.int32, (tb, tb), 0)
    ci = lax.broadcasted_iota(jnp.int32, (tb, tb), 1)
    tri = jnp.where((ri // CHUNK == ci // CHUNK) & (ci <= ri), 1.0, 0.0).astype(BF16)
    ld = ld_ref[...]
    hi = ld.astype(BF16)
    lo = (ld - hi.astype(F32)).astype(BF16)
    cum = _dot(tri, hi) + _dot(tri, lo)

    bi = lax.broadcasted_iota(jnp.int32, (gl, gl), 0)
    bj = lax.broadcasted_iota(jnp.int32, (gl, gl), 1)
    bmask = (bi // HEAD_SIZE) == (bj // HEAD_SIZE)
    trow = lax.broadcasted_iota(jnp.int32, (CHUNK, gl), 0)
    scol = lax.broadcasted_iota(jnp.int32, (CHUNK, gl), 1) % CHUNK
    strict = scol < trow
    incl = scol <= trow
    diag_cat = scol == trow
    eye_cat = jnp.where(diag_cat, 1.0, 0.0)
    bd = functools.partial(_block_diag, bmask=bmask)

    chunks = range(tb // CHUNK)
    sls = [slice(c * CHUNK, (c + 1) * CHUNK) for c in chunks]
    bdb = lambda x: bd(x).astype(BF16)

    lc = [cum[sl] for sl in sls]
    l_end = [x[CHUNK - 1:CHUNK, :] for x in lc]
    f32 = lambda ref, sl: ref[sl, :].astype(F32)
    v = [v_ref[sl, :] for sl in sls]
    b = [f32(b_ref, sl) for sl in sls]
    k = [f32(k_ref, sl) for sl in sls]
    rh = [f32(r_ref, sl) * jnp.exp(lc[c]) for c, sl in enumerate(sls)]
    ah = [-f32(kk_ref, sl) * jnp.exp(lc[c] - ld[sl]) for c, sl in enumerate(sls)]
    p_inv = [jnp.exp(-x) for x in lc]
    p_end = [jnp.exp(l_end[c] - lc[c]) for c in chunks]
    bh = [b[c] * p_inv[c] for c in chunks]
    kh = [k[c] * p_inv[c] for c in chunks]
    b_end = [(b[c] * p_end[c]).astype(BF16) for c in chunks]
    k_end = [(k[c] * p_end[c]).astype(BF16) for c in chunks]
    v_bd = [bdb(x.astype(F32)) for x in v]

    lhs = [jnp.concatenate([ah[c], rh[c]], axis=0).astype(BF16) for c in chunks]
    gb = [_dot_nt(lhs[c], bdb(bh[c])) for c in chunks]
    gk = [_dot_nt(lhs[c], bdb(kh[c])) for c in chunks]
    a_ab = [jnp.where(strict, x[:CHUNK], 0.0) for x in gb]
    a_rb = [jnp.where(incl, x[CHUNK:], 0.0).astype(BF16) for x in gb]
    a_ak = [jnp.where(strict, x[:CHUNK], 0.0).astype(BF16) for x in gk]
    a_rk = [jnp.where(incl, x[CHUNK:], 0.0).astype(BF16) for x in gk]

    tinv = [eye_cat + x for x in a_ab]
    pw = [_dot(x.astype(BF16), bdb(x)) for x in a_ab]
    for _ in range(CHUNK.bit_length() - 3):
        both = [_dot(jnp.concatenate([pw[c], tinv[c]], axis=0).astype(BF16), bdb(pw[c]))
                for c in chunks]
        tinv = [tinv[c] + both[c][CHUNK:] for c in chunks]
        pw = [x[:CHUNK] for x in both]
    tinv = [tinv[c] + _dot(tinv[c].astype(BF16), bdb(pw[c])) for c in chunks]
    tb16 = [x.astype(BF16) for x in tinv]

    w = [_dot(tb16[c], bdb(ah[c])) for c in chunks]
    zz = [_dot(a_ak[c], v_bd[c]) for c in chunks]
    ut = [_dot(tb16[c], bdb(zz[c])) for c in chunks]
    for c, sl in enumerate(sls):
        q_ref[sl, :] = (rh[c] + _dot(a_rb[c], bdb(w[c]))).astype(q_ref.dtype)
        y0_ref[sl, :] = _dot(a_rb[c], bdb(ut[c])) + _dot(a_rk[c], v_bd[c])

    for c in chunks:
        mt_bd = _dot_tn(w[c].astype(BF16), b_end[c])
        nt_bd = _dot_tn(ut[c].astype(BF16), b_end[c]) + _dot_tn(v[c].astype(BF16), k_end[c])
        mt_ref[c] = (jnp.where(diag_cat, jnp.exp(l_end[c]), 0.0) + _fold_rows(
            jnp.where(bmask, mt_bd, 0.0))).astype(mt_ref.dtype)
        nt_ref[c] = _fold_rows(jnp.where(bmask, nt_bd, 0.0))


def _scan_kernel(q_ref, y0_ref, mt_ref, nt_ref, g_ref, bonus_ref, lnw_ref, lnb_ref, hmean_ref,
                 y_ref, ht_ref, yraw_ref):
    gl = GROUP_LANES
    nb, tb, _ = q_ref.shape

    @pl.when(pl.program_id(1) == 0)
    def _():
        ht_ref[...] = jnp.zeros_like(ht_ref)

    bi = lax.broadcasted_iota(jnp.int32, (gl, gl), 0)
    bj = lax.broadcasted_iota(jnp.int32, (gl, gl), 1)
    bmask = (bi // HEAD_SIZE) == (bj // HEAD_SIZE)
    bdb = lambda x: _block_diag(x, bmask).astype(BF16)
    seqs = [(b, slice(gi * gl, (gi + 1) * gl)) for b in range(nb) for gi in range(N_GROUPS)]

    ht = [ht_ref[b, :, gs] for b, gs in seqs]
    for c in range(tb // CHUNK):
        sl = slice(c * CHUNK, (c + 1) * CHUNK)
        nxt = [_dot(ht[i].astype(BF16), bdb(mt_ref[b, c, :, gs])) + nt_ref[b, c, :, gs]
               for i, (b, gs) in enumerate(seqs)]
        for i, (b, gs) in enumerate(seqs):
            yraw_ref[b, sl, gs] = (_dot_nt(q_ref[b, sl, gs].astype(BF16), bdb(ht[i]))
                                   + y0_ref[b, sl, gs])
        ht = nxt
    for i, (b, gs) in enumerate(seqs):
        ht_ref[b, :, gs] = ht[i]

    for b, gs in seqs:
        y = yraw_ref[b, :, gs]
        d = y - _head_sums(y, hmean_ref[...])
        var = _head_sums(d * d, hmean_ref[...])
        yn = d * lax.rsqrt(var + GN_EPS) * lnw_ref[:, gs] + lnb_ref[:, gs]
        y_ref[b, :, gs] = ((yn + bonus_ref[b, :, gs]) * g_ref[b, :, gs]).astype(y_ref.dtype)


def _post_kernel(x1_ref, yrw_ref, ypool_ref, wout_ref, n2_ref, wg_ref, wu_ref, wd_ref, nf_ref,
                 o_ref):
    mix = _dot(yrw_ref[...].astype(BF16), wout_ref[0:RWKV_WIDTH, :])
    mix = mix + _dot(ypool_ref[...].astype(BF16), wout_ref[RWKV_WIDTH:, :])
    x2 = x1_ref[...] + mix
    h = _rms(x2, n2_ref[...]).astype(BF16)
    x3 = x2 + 0.5 * _swiglu(h, wg_ref, wu_ref, wd_ref)
    o_ref[...] = _rms(x3, nf_ref[...])


def _const_spec(shape):
    nd = len(shape)
    return pl.BlockSpec(shape, lambda *_: (0,) * nd, pipeline_mode=pl.Buffered(1))


def _head_block_ones(n, scale):
    i = jnp.arange(n) // HEAD_SIZE
    return jnp.where(i[:, None] == i[None, :], scale, 0.0).astype(BF16)


def kernel(x, ffn1_norm, ffn1_w_gate, ffn1_w_up, ffn1_w_down, mix_norm, w_in, mu_shift, w0, w_lora_up, a0, a_lora_up, g_lora_up, k_k, k_a, r_k, ln_w, ln_b, w_pool, pool_scale, w_out, ffn2_norm, ffn2_w_gate, ffn2_w_up, ffn2_w_down, final_norm):
    bsz, seq, _ = x.shape
    assert ffn1_norm.shape[0] == 1, "one trunk layer"
    tm = min(TOKEN_TILE, seq)
    tb = min(SCAN_TILE, seq)
    assert seq % tm == 0 and seq % tb == 0 and tb % CHUNK == 0 and tm >= POOL_HALO
    n_chunks = seq // CHUNK
    cpt = tb // CHUNK

    hsum = _head_block_ones(GROUP_LANES, 1.0)
    hmean = _head_block_ones(GROUP_LANES, 1.0 / HEAD_SIZE)
    row = lambda v: v.reshape(1, -1).astype(F32)

    tok = lambda width: pl.BlockSpec((None, tm, width), lambda b, t: (b, t, 0))
    tok_shape = lambda width, dtype=F32: jax.ShapeDtypeStruct((bsz, seq, width), dtype)

    w_in_p = jnp.concatenate(
        [w_in[0][:, :RWKV_COLS], jnp.zeros((D_MODEL, LORA_PAD - LORA_COLS), F32),
         w_in[0][:, RWKV_COLS:]], axis=1)
    mu_p = jnp.concatenate([mu_shift[0], jnp.zeros((LORA_PAD - LORA_COLS,), F32)])

    def lora_rows(w_up, start):
        pad = jnp.zeros((LORA_PAD, RWKV_WIDTH), F32)
        return pad.at[start:start + w_up.shape[0]].set(w_up).astype(BF16)

    l = 0
    if True:
        pre_in = [
            (x, tok(D_MODEL)),
            (row(ffn1_norm[l]), None),
            (ffn1_w_gate[l].astype(BF16), None),
            (ffn1_w_up[l].astype(BF16), None),
            (ffn1_w_down[l].astype(BF16), None),
            (row(mix_norm[l]), None),
            (w_in_p.astype(BF16), None),
            (row(mu_p), None),
            (row(w0[l]), None),
            (lora_rows(w_lora_up[l], 0), None),
            (row(a0[l]), None),
            (lora_rows(a_lora_up[l], D_DECAY_LORA), None),
            (lora_rows(g_lora_up[l], D_DECAY_LORA + D_AAA_LORA), None),
            (row(k_k[l]), None),
            (row(k_a[l]), None),
            (row(r_k[l]), None),
            (w_pool[l].astype(BF16), None),
            (row(pool_scale[l]), None),
            (hsum, None),
        ]
        pre_args = [a for a, _ in pre_in]
        pre_specs = [s if s is not None else _const_spec(a.shape) for a, s in pre_in]
        pre_dtypes = [BF16, BF16, BF16, BF16, BF16, F32, BF16, BF16, BF16]
        x1, r, k, v, kk, b, ld, g, bonus, ypool = pl.pallas_call(
            _pre_kernel,
            grid=(bsz, seq // tm),
            in_specs=pre_specs,
            out_specs=[tok(D_MODEL)] + [tok(RWKV_WIDTH)] * len(pre_dtypes),
            out_shape=[tok_shape(D_MODEL)] + [tok_shape(RWKV_WIDTH, d) for d in pre_dtypes],
            scratch_shapes=[pltpu.VMEM((8, SHIFT_COLS), F32),
                            pltpu.VMEM((POOL_HALO + tm, POOL_WIDTH), F32)],
            compiler_params=pltpu.CompilerParams(
                dimension_semantics=("arbitrary", "arbitrary"), vmem_limit_bytes=VMEM_LIMIT),
            name="pre",
        )(*pre_args)

        grp = pl.BlockSpec((None, tb, GROUP_LANES), lambda bb, gg, tt: (bb, tt, gg))
        mat = pl.BlockSpec((None, cpt, CHUNK, GROUP_LANES), lambda bb, gg, tt: (bb, tt, 0, gg))
        mat_shape = lambda dtype: jax.ShapeDtypeStruct((bsz, n_chunks, CHUNK, RWKV_WIDTH), dtype)
        q, y0, m, n = pl.pallas_call(
            _chunk_kernel,
            grid=(bsz, N_GROUPS, seq // tb),
            in_specs=[grp] * 6,
            out_specs=[grp, grp, mat, mat],
            out_shape=[tok_shape(RWKV_WIDTH, BF16), tok_shape(RWKV_WIDTH), mat_shape(BF16),
                       mat_shape(F32)],
            compiler_params=pltpu.CompilerParams(
                dimension_semantics=("arbitrary", "arbitrary", "arbitrary"),
                vmem_limit_bytes=VMEM_LIMIT),
            name="chunk",
        )(r, k, v, kk, b, ld)

        nb = SCAN_BATCH if bsz % SCAN_BATCH == 0 else 1
        tokb = pl.BlockSpec((nb, tb, RWKV_WIDTH), lambda bb, tt: (bb, tt, 0))
        matb = pl.BlockSpec((nb, cpt, CHUNK, RWKV_WIDTH), lambda bb, tt: (bb, tt, 0, 0))
        y_rw = pl.pallas_call(
            _scan_kernel,
            grid=(bsz // nb, seq // tb),
            in_specs=[tokb, tokb, matb, matb, tokb, tokb,
                      _const_spec((1, RWKV_WIDTH)), _const_spec((1, RWKV_WIDTH)),
                      _const_spec((GROUP_LANES, GROUP_LANES))],
            out_specs=tokb,
            out_shape=tok_shape(RWKV_WIDTH, BF16),
            scratch_shapes=[pltpu.VMEM((nb, HEAD_SIZE, RWKV_WIDTH), F32),
                            pltpu.VMEM((nb, tb, RWKV_WIDTH), F32)],
            compiler_params=pltpu.CompilerParams(
                dimension_semantics=("arbitrary", "arbitrary"), vmem_limit_bytes=VMEM_LIMIT),
            name="scan",
        )(q, y0, m, n, g, bonus, row(ln_w[l]), row(ln_b[l]), hmean)

        post_in = [
            (x1, tok(D_MODEL)),
            (y_rw, tok(RWKV_WIDTH)),
            (ypool, tok(POOL_WIDTH)),
            (w_out[l].astype(BF16), None),
            (row(ffn2_norm[l]), None),
            (ffn2_w_gate[l].astype(BF16), None),
            (ffn2_w_up[l].astype(BF16), None),
            (ffn2_w_down[l].astype(BF16), None),
            (row(final_norm), None),
        ]
        post_args = [a for a, _ in post_in]
        post_specs = [s if s is not None else _const_spec(a.shape) for a, s in post_in]
        x = pl.pallas_call(
            functools.partial(_post_kernel),
            grid=(bsz, seq // tm),
            in_specs=post_specs,
            out_specs=tok(D_MODEL),
            out_shape=tok_shape(D_MODEL),
            compiler_params=pltpu.CompilerParams(
                dimension_semantics=("arbitrary", "arbitrary"), vmem_limit_bytes=VMEM_LIMIT),
            name="post",
        )(*post_args)
    return x
```

```python
import functools

import jax
import jax.numpy as jnp
from jax import lax
from jax.experimental import pallas as pl
from jax.experimental.pallas import tpu as pltpu

F32 = jnp.float32
BF16 = jnp.bfloat16

D_MODEL = 1024
RWKV_WIDTH = 512
POOL_WIDTH = 512
HEAD_SIZE = 64
D_DECAY_LORA = 32
D_AAA_LORA = 32
D_GATE_LORA = 96
LORA_COLS = D_DECAY_LORA + D_AAA_LORA + D_GATE_LORA
LORA_PAD = 256
POOL_WINDOWS = (2, 4, 8, 16)
POOL_GROUP = 128
POOL_HALO = 16
D_FF = 2816
RWKV_COLS = 3 * RWKV_WIDTH + LORA_COLS
SHIFT_COLS = 3 * RWKV_WIDTH + LORA_PAD
RMS_EPS = 1e-6
GN_EPS = 64e-5
L2_EPS = 1e-12

CHUNK = 64
GROUP_LANES = 256
HEADS_PER_GROUP = GROUP_LANES // HEAD_SIZE
N_GROUPS = RWKV_WIDTH // GROUP_LANES
FF_CHUNK = 1408

TOKEN_TILE = 512
PRE_SPLIT = 2
SCAN_TILE = 512
SCAN_BATCH = 2
CUMSUM_ROWS = 256
VMEM_LIMIT = 56 * 1024 * 1024


def _dot(a, b):
    return jnp.dot(a, b, preferred_element_type=F32)


def _dot_nt(a, b):
    return lax.dot_general(a, b, (((1,), (1,)), ((), ())), preferred_element_type=F32)


def _head_sums(x, ones_bd):
    return _dot(x.astype(BF16), ones_bd)


def _rms(x, g):
    return x * lax.rsqrt(jnp.mean(x * x, axis=-1, keepdims=True) + RMS_EPS) * g


def _swiglu(hb, wg_ref, wu_ref, wd_ref):
    acc = None
    for c0 in range(0, D_FF, FF_CHUNK):
        gate = _dot(hb, wg_ref[:, c0:c0 + FF_CHUNK])
        up = _dot(hb, wu_ref[:, c0:c0 + FF_CHUNK])
        act = (gate * jax.nn.sigmoid(gate) * up).astype(BF16)
        part = _dot(act, wd_ref[c0:c0 + FF_CHUNK, :])
        acc = part if acc is None else acc + part
    return acc


def _pre_kernel(x_ref, n1_ref, wg_ref, wu_ref, wd_ref, nm_ref, win_ref, mu_ref, w0_ref,
                wlu_ref, a0_ref, alu_ref, glu_ref, kk_ref, ka_ref, rk_ref, wpool_ref,
                pscale_ref, hsum_ref,
                x1_ref, r_ref, k_ref, v_ref, kkn_ref, b_ref, ld_ref, g_ref, bonus_ref,
                ypool_ref, carry_ref, pool_ref):
    tm = x_ref.shape[0]
    t = pl.program_id(1)

    @pl.when(t == 0)
    def _():
        carry_ref[...] = jnp.zeros_like(carry_ref)
        pool_ref[0:POOL_HALO, :] = jnp.zeros((POOL_HALO, POOL_WIDTH), F32)

    def project(rows):
        x = x_ref[rows, :]
        h = _rms(x, n1_ref[...]).astype(BF16)
        x1 = x + 0.5 * _swiglu(h, wg_ref, wu_ref, wd_ref)
        x1_ref[rows, :] = x1
        h2 = _rms(x1, nm_ref[...]).astype(BF16)
        return _dot(h2, win_ref[...])

    def mix_inputs(rows, p):
        n = rows.stop - rows.start
        p_rw = p[:, :SHIFT_COLS]
        row = lax.broadcasted_iota(jnp.int32, (n, SHIFT_COLS), 0)
        prev = jnp.where(row == 0, carry_ref[0:1, :], pltpu.roll(p_rw, 1, 0))
        carry_ref[0:1, :] = p_rw[n - 1:n, :]
        p_rw = p_rw + (prev - p_rw) * mu_ref[...]

        p_r = p_rw[:, 0:RWKV_WIDTH]
        p_k = p_rw[:, RWKV_WIDTH:2 * RWKV_WIDTH]
        p_v = p_rw[:, 2 * RWKV_WIDTH:3 * RWKV_WIDTH]
        p_lora = p_rw[:, 3 * RWKV_WIDTH:SHIFT_COLS]

        hsum = hsum_ref[...]
        z = w0_ref[...] + _dot(jnp.tanh(p_lora).astype(BF16), wlu_ref[...])
        ld_ref[rows, :] = (-jnp.exp(-0.5)) * jax.nn.sigmoid(z)
        a = jax.nn.sigmoid(a0_ref[...] + _dot(p_lora.astype(BF16), alu_ref[...]))
        g_ref[rows, :] = _dot(jax.nn.sigmoid(p_lora).astype(BF16),
                              glu_ref[...]).astype(g_ref.dtype)

        kk = p_k * kk_ref[...]
        k = p_k * (1.0 + (a - 1.0) * ka_ref[...])
        rkr = p_r * k * rk_ref[...]
        r_ref[rows, :] = p_r.astype(r_ref.dtype)
        k_ref[rows, :] = k.astype(k_ref.dtype)
        v_ref[rows, :] = p_v.astype(v_ref.dtype)
        for gi in range(N_GROUPS):
            gs = slice(gi * GROUP_LANES, (gi + 1) * GROUP_LANES)
            kk_g = kk[:, gs]
            kk_g = kk_g / jnp.maximum(jnp.sqrt(_head_sums(kk_g * kk_g, hsum)), L2_EPS)
            kkn_ref[rows, gs] = kk_g.astype(kkn_ref.dtype)
            b_ref[rows, gs] = (kk_g * a[:, gs]).astype(b_ref.dtype)
            bonus_ref[rows, gs] = (_head_sums(rkr[:, gs], hsum) * p_v[:, gs]).astype(bonus_ref.dtype)

        base = POOL_HALO + rows.start
        pool_ref[base:base + n, :] = p[:, SHIFT_COLS:]
        pos = t * tm + rows.start + lax.broadcasted_iota(jnp.int32, (n, POOL_GROUP), 0) + 1
        for gi, win in enumerate(POOL_WINDOWS):
            cols = slice(gi * POOL_GROUP, (gi + 1) * POOL_GROUP)
            cur = pool_ref[base:base + n, cols]
            wsum = cur
            for j in range(1, win):
                wsum = wsum + pool_ref[base - j:base - j + n, cols]
            count = jnp.minimum(pos, win).astype(F32)
            pooled = wsum / count - cur
            mixed = _dot(pooled.astype(BF16), wpool_ref[gi])
            ypool_ref[rows, cols] = (mixed * pscale_ref[:, cols]).astype(ypool_ref.dtype)

    blocks = [slice(i, i + tm // PRE_SPLIT) for i in range(0, tm, tm // PRE_SPLIT)]
    ps = [project(rows) for rows in blocks]
    for rows, p in zip(blocks, ps):
        mix_inputs(rows, p)
    pool_ref[0:POOL_HALO, :] = pool_ref[tm:tm + POOL_HALO, :]


def _block_diag(x, bmask):
    return jnp.where(bmask, jnp.concatenate([x] * HEADS_PER_GROUP, axis=0), 0.0)


def _head_transpose(x):
    half = GROUP_LANES // 2
    y = jnp.concatenate([x[:, :half], x[:, half:]], axis=0)
    yt = y.T
    return jnp.concatenate([yt[:HEAD_SIZE], yt[HEAD_SIZE:]], axis=1)


def _perm_head(block):
    return (block % 2) * 2 + block // 2


def _chunk_kernel(r_ref, k_ref, v_ref, kk_ref, b_ref, ld_ref, q_ref, y0_ref, mt_ref, nt_ref):
    tb = r_ref.shape[0]
    gl = GROUP_LANES

    cb = min(tb, CUMSUM_ROWS)
    ri = lax.broadcasted_iota(jnp.int32, (cb, cb), 0)
    ci = lax.broadcasted_iota(jnp.int32, (cb, cb), 1)
    tri = jnp.where((ri // CHUNK == ci // CHUNK) & (ci <= ri), 1.0, 0.0).astype(BF16)
    ld16 = ld_ref[...].astype(BF16)
    ld = ld16.astype(F32)
    cum = jnp.concatenate([_dot(tri, ld16[i:i + cb]) for i in range(0, tb, cb)], axis=0)

    bi = lax.broadcasted_iota(jnp.int32, (gl, gl), 0)
    bj = lax.broadcasted_iota(jnp.int32, (gl, gl), 1)
    bmask = (bi // HEAD_SIZE) == (bj // HEAD_SIZE)
    trow = lax.broadcasted_iota(jnp.int32, (CHUNK, gl), 0)
    scol = lax.broadcasted_iota(jnp.int32, (CHUNK, gl), 1) % CHUNK
    strict = scol < trow
    incl = scol <= trow
    diag_cat = scol == trow
    eye_cat = jnp.where(diag_cat, 1.0, 0.0)
    bd = functools.partial(_block_diag, bmask=bmask)

    chunks = range(tb // CHUNK)
    sls = [slice(c * CHUNK, (c + 1) * CHUNK) for c in chunks]
    bdb = lambda x: bd(x).astype(BF16)

    lc = [cum[sl] for sl in sls]
    l_end = [x[CHUNK - 1:CHUNK, :] for x in lc]
    f32 = lambda ref, sl: ref[sl, :].astype(F32)
    v = [v_ref[sl, :] for sl in sls]
    b = [f32(b_ref, sl) for sl in sls]
    k = [f32(k_ref, sl) for sl in sls]
    rh = [f32(r_ref, sl) * jnp.exp(lc[c]) for c, sl in enumerate(sls)]
    ah = [-f32(kk_ref, sl) * jnp.exp(lc[c] - ld[sl]) for c, sl in enumerate(sls)]
    p_inv = [jnp.exp(-x) for x in lc]
    p_end = [jnp.exp(l_end[c] - lc[c]) for c in chunks]
    bh = [b[c] * p_inv[c] for c in chunks]
    kh = [k[c] * p_inv[c] for c in chunks]
    b_end = [(b[c] * p_end[c]).astype(BF16) for c in chunks]
    k_end = [(k[c] * p_end[c]).astype(BF16) for c in chunks]
    v_bd = [bdb(x.astype(F32)) for x in v]

    lhs = [jnp.concatenate([ah[c], rh[c]], axis=0).astype(BF16) for c in chunks]
    gb = [_dot_nt(lhs[c], bdb(bh[c])) for c in chunks]
    gk = [_dot_nt(lhs[c], bdb(kh[c])) for c in chunks]
    a_ab = [jnp.where(strict, x[:CHUNK], 0.0) for x in gb]
    a_rb = [jnp.where(incl, x[CHUNK:], 0.0).astype(BF16) for x in gb]
    a_ak = [jnp.where(strict, x[:CHUNK], 0.0).astype(BF16) for x in gk]
    a_rk = [jnp.where(incl, x[CHUNK:], 0.0).astype(BF16) for x in gk]

    tinv = [eye_cat + x for x in a_ab]
    pw = [_dot(x.astype(BF16), bdb(x)) for x in a_ab]
    for _ in range(CHUNK.bit_length() - 3):
        both = [_dot(jnp.concatenate([pw[c], tinv[c]], axis=0).astype(BF16), bdb(pw[c]))
                for c in chunks]
        tinv = [tinv[c] + both[c][CHUNK:] for c in chunks]
        pw = [x[:CHUNK] for x in both]
    tinv = [tinv[c] + _dot(tinv[c].astype(BF16), bdb(pw[c])) for c in chunks]
    tb16 = [x.astype(BF16) for x in tinv]

    w = [_dot(tb16[c], bdb(ah[c])) for c in chunks]
    zz = [_dot(a_ak[c], v_bd[c]) for c in chunks]
    ut = [_dot(tb16[c], bdb(zz[c])) for c in chunks]
    for c, sl in enumerate(sls):
        q_ref[sl, :] = (rh[c] + _dot(a_rb[c], bdb(w[c]))).astype(q_ref.dtype)
        y0_ref[sl, :] = _dot(a_rb[c], bdb(ut[c])) + _dot(a_rk[c], v_bd[c])

    pmask = _perm_head(bi // HEAD_SIZE) == (bj // HEAD_SIZE)
    bdp = lambda x: jnp.where(pmask, jnp.concatenate([x] * HEADS_PER_GROUP, axis=0), 0.0)
    wu_t = [jnp.concatenate([_head_transpose(w[c]), _head_transpose(ut[c])], axis=0).astype(BF16)
            for c in chunks]
    v_t = [_head_transpose(x.astype(F32)).astype(BF16) for x in v]
    for c in chunks:
        mn = _dot(wu_t[c], bdp(b_end[c]))
        nt_ref[c] = mn[HEAD_SIZE:] + _dot(v_t[c], bdp(k_end[c]))
        mt_ref[c] = (jnp.where(diag_cat, jnp.exp(l_end[c]), 0.0)
                     + mn[:HEAD_SIZE]).astype(mt_ref.dtype)


def _scan_kernel(q_ref, y0_ref, mt_ref, nt_ref, g_ref, bonus_ref, lnw_ref, lnb_ref, hmean_ref,
                 y_ref, ht_ref, yraw_ref):
    gl = GROUP_LANES
    nb, tb, _ = q_ref.shape

    @pl.when(pl.program_id(1) == 0)
    def _():
        ht_ref[...] = jnp.zeros_like(ht_ref)

    bi = lax.broadcasted_iota(jnp.int32, (gl, gl), 0)
    bj = lax.broadcasted_iota(jnp.int32, (gl, gl), 1)
    bmask = (bi // HEAD_SIZE) == (bj // HEAD_SIZE)
    bdb = lambda x: _block_diag(x, bmask).astype(BF16)
    seqs = [(b, slice(gi * gl, (gi + 1) * gl)) for b in range(nb) for gi in range(N_GROUPS)]

    ht = [ht_ref[b, :, gs] for b, gs in seqs]
    for c in range(tb // CHUNK):
        sl = slice(c * CHUNK, (c + 1) * CHUNK)
        nxt = [_dot(ht[i].astype(BF16), bdb(mt_ref[b, c, :, gs])) + nt_ref[b, c, :, gs]
               for i, (b, gs) in enumerate(seqs)]
        for i, (b, gs) in enumerate(seqs):
            yraw_ref[b, sl, gs] = (_dot_nt(q_ref[b, sl, gs].astype(BF16), bdb(ht[i]))
                                   + y0_ref[b, sl, gs])
        ht = nxt
    for i, (b, gs) in enumerate(seqs):
        ht_ref[b, :, gs] = ht[i]

    for b, gs in seqs:
        y = yraw_ref[b, :, gs]
        d = y - _head_sums(y, hmean_ref[...])
        var = _head_sums(d * d, hmean_ref[...])
        yn = d * lax.rsqrt(var + GN_EPS) * lnw_ref[:, gs] + lnb_ref[:, gs]
        y_ref[b, :, gs] = ((yn + bonus_ref[b, :, gs]) * g_ref[b, :, gs]).astype(y_ref.dtype)


def _post_kernel(x1_ref, yrw_ref, ypool_ref, wout_ref, n2_ref, wg_ref, wu_ref, wd_ref, nf_ref,
                 o_ref):
    mix = _dot(yrw_ref[...].astype(BF16), wout_ref[0:RWKV_WIDTH, :])
    mix = mix + _dot(ypool_ref[...].astype(BF16), wout_ref[RWKV_WIDTH:, :])
    x2 = x1_ref[...] + mix
    h = _rms(x2, n2_ref[...]).astype(BF16)
    x3 = x2 + 0.5 * _swiglu(h, wg_ref, wu_ref, wd_ref)
    o_ref[...] = _rms(x3, nf_ref[...])


def _const_spec(shape):
    nd = len(shape)
    return pl.BlockSpec(shape, lambda *_: (0,) * nd, pipeline_mode=pl.Buffered(1))


def _head_block_ones(n, scale):
    i = jnp.arange(n) // HEAD_SIZE
    return jnp.where(i[:, None] == i[None, :], scale, 0.0).astype(BF16)


def kernel(x, ffn1_norm, ffn1_w_gate, ffn1_w_up, ffn1_w_down, mix_norm, w_in, mu_shift, w0, w_lora_up, a0, a_lora_up, g_lora_up, k_k, k_a, r_k, ln_w, ln_b, w_pool, pool_scale, w_out, ffn2_norm, ffn2_w_gate, ffn2_w_up, ffn2_w_down, final_norm):
    bsz, seq, _ = x.shape
    assert ffn1_norm.shape[0] == 1, "one trunk layer"
    tm = min(TOKEN_TILE, seq)
    tb = min(SCAN_TILE, seq)
    assert seq % tm == 0 and seq % tb == 0 and tb % CHUNK == 0 and tm >= POOL_HALO
    n_chunks = seq // CHUNK
    cpt = tb // CHUNK

    hsum = _head_block_ones(GROUP_LANES, 1.0)
    hmean = _head_block_ones(GROUP_LANES, 1.0 / HEAD_SIZE)
    row = lambda v: v.reshape(1, -1).astype(F32)

    tok = lambda width: pl.BlockSpec((None, tm, width), lambda b, t: (b, t, 0))
    tok_shape = lambda width, dtype=F32: jax.ShapeDtypeStruct((bsz, seq, width), dtype)

    w_in_p = jnp.concatenate(
        [w_in[0][:, :RWKV_COLS], jnp.zeros((D_MODEL, LORA_PAD - LORA_COLS), F32),
         w_in[0][:, RWKV_COLS:]], axis=1)
    mu_p = jnp.concatenate([mu_shift[0], jnp.zeros((LORA_PAD - LORA_COLS,), F32)])

    def lora_rows(w_up, start):
        pad = jnp.zeros((LORA_PAD, RWKV_WIDTH), F32)
        return pad.at[start:start + w_up.shape[0]].set(w_up).astype(BF16)

    l = 0
    if True:
        pre_in = [
            (x, tok(D_MODEL)),
            (row(ffn1_norm[l]), None),
            (ffn1_w_gate[l].astype(BF16), None),
            (ffn1_w_up[l].astype(BF16), None),
            (ffn1_w_down[l].astype(BF16), None),
            (row(mix_norm[l]), None),
            (w_in_p.astype(BF16), None),
            (row(mu_p), None),
            (row(w0[l]), None),
            (lora_rows(w_lora_up[l], 0), None),
            (row(a0[l]), None),
            (lora_rows(a_lora_up[l], D_DECAY_LORA), None),
            (lora_rows(g_lora_up[l], D_DECAY_LORA + D_AAA_LORA), None),
            (row(k_k[l]), None),
            (row(k_a[l]), None),
            (row(r_k[l]), None),
            (w_pool[l].astype(BF16), None),
            (row(pool_scale[l]), None),
            (hsum, None),
        ]
        pre_args = [a for a, _ in pre_in]
        pre_specs = [s if s is not None else _const_spec(a.shape) for a, s in pre_in]
        pre_dtypes = [BF16, BF16, BF16, BF16, BF16, F32, BF16, BF16, BF16]
        x1, r, k, v, kk, b, ld, g, bonus, ypool = pl.pallas_call(
            _pre_kernel,
            grid=(bsz, seq // tm),
            in_specs=pre_specs,
            out_specs=[tok(D_MODEL)] + [tok(RWKV_WIDTH)] * len(pre_dtypes),
            out_shape=[tok_shape(D_MODEL)] + [tok_shape(RWKV_WIDTH, d) for d in pre_dtypes],
            scratch_shapes=[pltpu.VMEM((8, SHIFT_COLS), F32),
                            pltpu.VMEM((POOL_HALO + tm, POOL_WIDTH), F32)],
            compiler_params=pltpu.CompilerParams(
                dimension_semantics=("arbitrary", "arbitrary"), vmem_limit_bytes=VMEM_LIMIT),
            name="pre",
        )(*pre_args)

        grp = pl.BlockSpec((None, tb, GROUP_LANES), lambda bb, gg, tt: (bb, tt, gg))
        mat = pl.BlockSpec((None, cpt, CHUNK, GROUP_LANES), lambda bb, gg, tt: (bb, tt, 0, gg))
        mat_shape = lambda dtype: jax.ShapeDtypeStruct((bsz, n_chunks, CHUNK, RWKV_WIDTH), dtype)
        q, y0, m, n = pl.pallas_call(
            _chunk_kernel,
            grid=(bsz, N_GROUPS, seq // tb),
            in_specs=[grp] * 6,
            out_specs=[grp, grp, mat, mat],
            out_shape=[tok_shape(RWKV_WIDTH, BF16), tok_shape(RWKV_WIDTH), mat_shape(BF16),
                       mat_shape(F32)],
            compiler_params=pltpu.CompilerParams(
                dimension_semantics=("arbitrary", "arbitrary", "arbitrary"),
                vmem_limit_bytes=VMEM_LIMIT),
            name="chunk",
        )(r, k, v, kk, b, ld)

        nb = SCAN_BATCH if bsz % SCAN_BATCH == 0 else 1
        tokb = pl.BlockSpec((nb, tb, RWKV_WIDTH), lambda bb, tt: (bb, tt, 0))
        matb = pl.BlockSpec((nb, cpt, CHUNK, RWKV_WIDTH), lambda bb, tt: (bb, tt, 0, 0))
        y_rw = pl.pallas_call(
            _scan_kernel,
            grid=(bsz // nb, seq // tb),
            in_specs=[tokb, tokb, matb, matb, tokb, tokb,
                      _const_spec((1, RWKV_WIDTH)), _const_spec((1, RWKV_WIDTH)),
                      _const_spec((GROUP_LANES, GROUP_LANES))],
            out_specs=tokb,
            out_shape=tok_shape(RWKV_WIDTH, BF16),
            scratch_shapes=[pltpu.VMEM((nb, HEAD_SIZE, RWKV_WIDTH), F32),
                            pltpu.VMEM((nb, tb, RWKV_WIDTH), F32)],
            compiler_params=pltpu.CompilerParams(
                dimension_semantics=("arbitrary", "arbitrary"), vmem_limit_bytes=VMEM_LIMIT),
            name="scan",
        )(q, y0, m, n, g, bonus, row(ln_w[l]), row(ln_b[l]), hmean)

        post_in = [
            (x1, tok(D_MODEL)),
            (y_rw, tok(RWKV_WIDTH)),
            (ypool, tok(POOL_WIDTH)),
            (w_out[l].astype(BF16), None),
            (row(ffn2_norm[l]), None),
            (ffn2_w_gate[l].astype(BF16), None),
            (ffn2_w_up[l].astype(BF16), None),
            (ffn2_w_down[l].astype(BF16), None),
            (row(final_norm), None),
        ]
        post_args = [a for a, _ in post_in]
        post_specs = [s if s is not None else _const_spec(a.shape) for a, s in post_in]
        x = pl.pallas_call(
            functools.partial(_post_kernel),
            grid=(bsz, seq // tm),
            in_specs=post_specs,
            out_specs=tok(D_MODEL),
            out_shape=tok_shape(D_MODEL),
            compiler_params=pltpu.CompilerParams(
                dimension_semantics=("arbitrary", "arbitrary"), vmem_limit_bytes=VMEM_LIMIT),
            name="post",
        )(*post_args)
    return x
```

```python
import functools

import jax
import jax.numpy as jnp
from jax import lax
from jax.experimental import pallas as pl
from jax.experimental.pallas import tpu as pltpu

F32 = jnp.float32
BF16 = jnp.bfloat16

D_MODEL = 1024
RWKV_WIDTH = 512
POOL_WIDTH = 512
HEAD_SIZE = 64
D_DECAY_LORA = 32
D_AAA_LORA = 32
D_GATE_LORA = 96
LORA_COLS = D_DECAY_LORA + D_AAA_LORA + D_GATE_LORA
LORA_PAD = 256
POOL_WINDOWS = (2, 4, 8, 16)
POOL_GROUP = 128
POOL_HALO = 16
D_FF = 2816
RWKV_COLS = 3 * RWKV_WIDTH + LORA_COLS
SHIFT_COLS = 3 * RWKV_WIDTH + LORA_PAD
RMS_EPS = 1e-6
GN_EPS = 64e-5
L2_EPS = 1e-12

CHUNK = 64
GROUP_LANES = 256
HEADS_PER_GROUP = GROUP_LANES // HEAD_SIZE
N_GROUPS = RWKV_WIDTH // GROUP_LANES
MXU_TILE = 256
FF_CHUNKS = (6 * MXU_TILE, 5 * MXU_TILE)
assert sum(FF_CHUNKS) == D_FF

TOKEN_TILE = 512
PRE_BLOCKS = 4
PRE_BLOCK_LAG = 1
POST_BLOCKS = 1
POST_BLOCK_LAG = 0
SCAN_TILE = 1024
SCAN_BATCH = 2
CUMSUM_ROWS = 256
CHUNK_GROUPS = 1
CHUNK_GROUP_LAG = 0
VMEM_LIMIT = 56 * 1024 * 1024


def _dot(a, b):
    return jnp.dot(a, b, preferred_element_type=F32)


def _dot_nt(a, b):
    return lax.dot_general(a, b, (((1,), (1,)), ((), ())), preferred_element_type=F32)


def _interleave(gens, lag=0):
    live = list(enumerate(gens))
    rnd = 0
    while live:
        for g, gen in list(live):
            if rnd >= g * lag and next(gen, StopIteration) is StopIteration:
                live.remove((g, gen))
        rnd += 1


def _head_sums(x, ones_bd):
    return _dot(x.astype(BF16), ones_bd)


def _rms(x, g):
    return x * lax.rsqrt(jnp.mean(x * x, axis=-1, keepdims=True) + RMS_EPS) * g


def _swiglu_parts(hb, wg_ref, wu_ref, wd_ref):
    c0 = 0
    for width in FF_CHUNKS:
        gate = _dot(hb, wg_ref[:, c0:c0 + width])
        up = _dot(hb, wu_ref[:, c0:c0 + width])
        act = (gate * jax.nn.sigmoid(gate) * up).astype(BF16)
        yield _dot(act, wd_ref[c0:c0 + width, :])
        c0 += width


def _swiglu(hb, wg_ref, wu_ref, wd_ref):
    return functools.reduce(lambda a, b: a + b, _swiglu_parts(hb, wg_ref, wu_ref, wd_ref))


def _pre_kernel(x_ref, n1_ref, wg_ref, wu_ref, wd_ref, nm_ref, win_ref, mu_ref, w0_ref,
                wlu_ref, a0_ref, alu_ref, glu_ref, kk_ref, ka_ref, rk_ref, wpool_ref,
                pscale_ref, hsum_ref,
                x1_ref, r_ref, k_ref, v_ref, kkn_ref, b_ref, ld_ref, g_ref, bonus_ref,
                ypool_ref, carry_ref, pool_ref):
    tm = x_ref.shape[0]
    t = pl.program_id(1)

    @pl.when(t == 0)
    def _():
        carry_ref[...] = jnp.zeros_like(carry_ref)
        pool_ref[0:POOL_HALO, :] = jnp.zeros((POOL_HALO, POOL_WIDTH), F32)

    def block(rows):
        n = rows.stop - rows.start
        x = x_ref[rows, :]
        h = _rms(x, n1_ref[...]).astype(BF16)
        yield
        acc = None
        for part in _swiglu_parts(h, wg_ref, wu_ref, wd_ref):
            acc = part if acc is None else acc + part
            yield
        x1 = x + 0.5 * acc
        x1_ref[rows, :] = x1
        h2 = _rms(x1, nm_ref[...]).astype(BF16)
        p = _dot(h2, win_ref[...])
        yield

        p_rw = p[:, :SHIFT_COLS]
        row = lax.broadcasted_iota(jnp.int32, (n, SHIFT_COLS), 0)
        prev = jnp.where(row == 0, carry_ref[0:1, :], pltpu.roll(p_rw, 1, 0))
        carry_ref[0:1, :] = p_rw[n - 1:n, :]
        p_rw = p_rw + (prev - p_rw) * mu_ref[...]

        p_r = p_rw[:, 0:RWKV_WIDTH]
        p_k = p_rw[:, RWKV_WIDTH:2 * RWKV_WIDTH]
        p_v = p_rw[:, 2 * RWKV_WIDTH:3 * RWKV_WIDTH]
        p_lora = p_rw[:, 3 * RWKV_WIDTH:SHIFT_COLS]

        hsum = hsum_ref[...]
        z = w0_ref[...] + _dot(jnp.tanh(p_lora).astype(BF16), wlu_ref[...])
        ld_ref[rows, :] = (-jnp.exp(-0.5)) * jax.nn.sigmoid(z)
        a = jax.nn.sigmoid(a0_ref[...] + _dot(p_lora.astype(BF16), alu_ref[...]))
        g_ref[rows, :] = _dot(jax.nn.sigmoid(p_lora).astype(BF16),
                              glu_ref[...]).astype(g_ref.dtype)

        kk = p_k * kk_ref[...]
        k = p_k * (1.0 + (a - 1.0) * ka_ref[...])
        rkr = p_r * k * rk_ref[...]
        r_ref[rows, :] = p_r.astype(r_ref.dtype)
        k_ref[rows, :] = k.astype(k_ref.dtype)
        v_ref[rows, :] = p_v.astype(v_ref.dtype)
        yield
        for gi in range(N_GROUPS):
            gs = slice(gi * GROUP_LANES, (gi + 1) * GROUP_LANES)
            kk_g = kk[:, gs]
            kk_g = kk_g / jnp.maximum(jnp.sqrt(_head_sums(kk_g * kk_g, hsum)), L2_EPS)
            kkn_ref[rows, gs] = kk_g.astype(kkn_ref.dtype)
            b_ref[rows, gs] = (kk_g * a[:, gs]).astype(b_ref.dtype)
            bonus_ref[rows, gs] = (_head_sums(rkr[:, gs], hsum) * p_v[:, gs]).astype(bonus_ref.dtype)
        yield

        base = POOL_HALO + rows.start
        pool_ref[base:base + n, :] = p[:, SHIFT_COLS:]
        pos = t * tm + rows.start + lax.broadcasted_iota(jnp.int32, (n, POOL_GROUP), 0) + 1
        for gi, win in enumerate(POOL_WINDOWS):
            cols = slice(gi * POOL_GROUP, (gi + 1) * POOL_GROUP)
            cur = pool_ref[base:base + n, cols]
            wsum = cur
            for j in range(1, win):
                wsum = wsum + pool_ref[base - j:base - j + n, cols]
            count = jnp.minimum(pos, win).astype(F32)
            pooled = wsum / count - cur
            mixed = _dot(pooled.astype(BF16), wpool_ref[gi])
            ypool_ref[rows, cols] = (mixed * pscale_ref[:, cols]).astype(ypool_ref.dtype)

    step = tm // PRE_BLOCKS
    _interleave([block(slice(i, i + step)) for i in range(0, tm, step)], PRE_BLOCK_LAG)
    pool_ref[0:POOL_HALO, :] = pool_ref[tm:tm + POOL_HALO, :]


def _block_diag(x, bmask):
    return jnp.where(bmask, jnp.concatenate([x] * HEADS_PER_GROUP, axis=0), 0.0)


def _head_transpose(x):
    half = GROUP_LANES // 2
    y = jnp.concatenate([x[:, :half], x[:, half:]], axis=0)
    yt = y.T
    return jnp.concatenate([yt[:HEAD_SIZE], yt[HEAD_SIZE:]], axis=1)


def _perm_head(block):
    return (block % 2) * 2 + block // 2


def _chunk_kernel(r_ref, k_ref, v_ref, kk_ref, b_ref, ld_ref, q_ref, y0_ref, mt_ref, nt_ref):
    tb = r_ref.shape[0]
    gl = GROUP_LANES

    cb = min(tb, CUMSUM_ROWS)
    ri = lax.broadcasted_iota(jnp.int32, (cb, cb), 0)
    ci = lax.broadcasted_iota(jnp.int32, (cb, cb), 1)
    tri = jnp.where((ri // CHUNK == ci // CHUNK) & (ci <= ri), 1.0, 0.0).astype(BF16)
    ld16 = ld_ref[...].astype(BF16)
    ld = ld16.astype(F32)
    cum = jnp.concatenate([_dot(tri, ld16[i:i + cb]) for i in range(0, tb, cb)], axis=0)

    bi = lax.broadcasted_iota(jnp.int32, (gl, gl), 0)
    bj = lax.broadcasted_iota(jnp.int32, (gl, gl), 1)
    bmask = (bi // HEAD_SIZE) == (bj // HEAD_SIZE)
    trow = lax.broadcasted_iota(jnp.int32, (CHUNK, gl), 0)
    scol = lax.broadcasted_iota(jnp.int32, (CHUNK, gl), 1) % CHUNK
    strict = scol < trow
    incl = scol <= trow
    diag_cat = scol == trow
    eye_cat = jnp.where(diag_cat, 1.0, 0.0)
    bd = functools.partial(_block_diag, bmask=bmask)

    bdb = lambda x: bd(x).astype(BF16)
    pmask = _perm_head(bi // HEAD_SIZE) == (bj // HEAD_SIZE)
    bdp = lambda x: jnp.where(pmask, jnp.concatenate([x] * HEADS_PER_GROUP, axis=0), 0.0)
    f32 = lambda ref, sl: ref[sl, :].astype(F32)

    def chain(ids):
        chunks = range(len(ids))
        sls = [slice(c * CHUNK, (c + 1) * CHUNK) for c in ids]
        lc = [cum[sl] for sl in sls]
        l_end = [x[CHUNK - 1:CHUNK, :] for x in lc]
        v = [v_ref[sl, :] for sl in sls]
        b = [f32(b_ref, sl) for sl in sls]
        k = [f32(k_ref, sl) for sl in sls]
        rh = [f32(r_ref, sl) * jnp.exp(lc[c]) for c, sl in enumerate(sls)]
        ah = [-f32(kk_ref, sl) * jnp.exp(lc[c] - ld[sl]) for c, sl in enumerate(sls)]
        p_inv = [jnp.exp(-x) for x in lc]
        p_end = [jnp.exp(l_end[c] - lc[c]) for c in chunks]
        bh = [b[c] * p_inv[c] for c in chunks]
        kh = [k[c] * p_inv[c] for c in chunks]
        b_end = [(b[c] * p_end[c]).astype(BF16) for c in chunks]
        k_end = [(k[c] * p_end[c]).astype(BF16) for c in chunks]
        yield
        v_bd = [bdb(x.astype(F32)) for x in v]
        lhs = [jnp.concatenate([ah[c], rh[c]], axis=0).astype(BF16) for c in chunks]
        gb = [_dot_nt(lhs[c], bdb(bh[c])) for c in chunks]
        gk = [_dot_nt(lhs[c], bdb(kh[c])) for c in chunks]
        a_ab = [jnp.where(strict, x[:CHUNK], 0.0) for x in gb]
        a_rb = [jnp.where(incl, x[CHUNK:], 0.0).astype(BF16) for x in gb]
        a_ak = [jnp.where(strict, x[:CHUNK], 0.0).astype(BF16) for x in gk]
        a_rk = [jnp.where(incl, x[CHUNK:], 0.0).astype(BF16) for x in gk]
        yield

        tinv = [eye_cat + x for x in a_ab]
        pw = [_dot(x.astype(BF16), bdb(x)) for x in a_ab]
        yield
        for _ in range(CHUNK.bit_length() - 3):
            both = [_dot(jnp.concatenate([pw[c], tinv[c]], axis=0).astype(BF16), bdb(pw[c]))
                    for c in chunks]
            tinv = [tinv[c] + both[c][CHUNK:] for c in chunks]
            pw = [x[:CHUNK] for x in both]
            yield
        tinv = [tinv[c] + _dot(tinv[c].astype(BF16), bdb(pw[c])) for c in chunks]
        tb16 = [x.astype(BF16) for x in tinv]
        yield

        w = [_dot(tb16[c], bdb(ah[c])) for c in chunks]
        zz = [_dot(a_ak[c], v_bd[c]) for c in chunks]
        ut = [_dot(tb16[c], bdb(zz[c])) for c in chunks]
        yield
        for c, sl in enumerate(sls):
            q_ref[sl, :] = (rh[c] + _dot(a_rb[c], bdb(w[c]))).astype(q_ref.dtype)
            y0_ref[sl, :] = _dot(a_rb[c], bdb(ut[c])) + _dot(a_rk[c], v_bd[c])
        yield

        wu_t = [jnp.concatenate([_head_transpose(w[c]), _head_transpose(ut[c])],
                                axis=0).astype(BF16) for c in chunks]
        v_t = [_head_transpose(x.astype(F32)).astype(BF16) for x in v]
        for c in chunks:
            mn = _dot(wu_t[c], bdp(b_end[c]))
            nt_ref[ids[c]] = mn[HEAD_SIZE:] + _dot(v_t[c], bdp(k_end[c]))
            mt_ref[ids[c]] = (jnp.where(diag_cat, jnp.exp(l_end[c]), 0.0)
                              + mn[:HEAD_SIZE]).astype(mt_ref.dtype)

    n_chunks = tb // CHUNK
    per = max(1, n_chunks // CHUNK_GROUPS)
    _interleave([chain(list(range(s, min(s + per, n_chunks)))) for s in range(0, n_chunks, per)],
                CHUNK_GROUP_LAG)


def _scan_kernel(q_ref, y0_ref, mt_ref, nt_ref, g_ref, bonus_ref, lnw_ref, lnb_ref, hmean_ref,
                 y_ref, ht_ref, yraw_ref):
    gl = GROUP_LANES
    nb, tb, _ = q_ref.shape

    @pl.when(pl.program_id(1) == 0)
    def _():
        ht_ref[...] = jnp.zeros_like(ht_ref)

    bi = lax.broadcasted_iota(jnp.int32, (gl, gl), 0)
    bj = lax.broadcasted_iota(jnp.int32, (gl, gl), 1)
    bmask = (bi // HEAD_SIZE) == (bj // HEAD_SIZE)
    bdb = lambda x: _block_diag(x, bmask).astype(BF16)
    seqs = [(b, slice(gi * gl, (gi + 1) * gl)) for b in range(nb) for gi in range(N_GROUPS)]

    ht = [ht_ref[b, :, gs] for b, gs in seqs]
    for c in range(tb // CHUNK):
        sl = slice(c * CHUNK, (c + 1) * CHUNK)
        nxt = [_dot(ht[i].astype(BF16), bdb(mt_ref[b, c, :, gs])) + nt_ref[b, c, :, gs]
               for i, (b, gs) in enumerate(seqs)]
        for i, (b, gs) in enumerate(seqs):
            yraw_ref[b, sl, gs] = (_dot_nt(q_ref[b, sl, gs].astype(BF16), bdb(ht[i]))
                                   + y0_ref[b, sl, gs])
        ht = nxt
    for i, (b, gs) in enumerate(seqs):
        ht_ref[b, :, gs] = ht[i]

    for b, gs in seqs:
        y = yraw_ref[b, :, gs]
        d = y - _head_sums(y, hmean_ref[...])
        var = _head_sums(d * d, hmean_ref[...])
        yn = d * lax.rsqrt(var + GN_EPS) * lnw_ref[:, gs] + lnb_ref[:, gs]
        y_ref[b, :, gs] = ((yn + bonus_ref[b, :, gs]) * g_ref[b, :, gs]).astype(y_ref.dtype)


def _post_kernel(x1_ref, yrw_ref, ypool_ref, wout_ref, n2_ref, wg_ref, wu_ref, wd_ref, nf_ref,
                 o_ref):
    tm = x1_ref.shape[0]

    def block(rows):
        mix = _dot(yrw_ref[rows, :].astype(BF16), wout_ref[0:RWKV_WIDTH, :])
        mix = mix + _dot(ypool_ref[rows, :].astype(BF16), wout_ref[RWKV_WIDTH:, :])
        x2 = x1_ref[rows, :] + mix
        h = _rms(x2, n2_ref[...]).astype(BF16)
        yield
        acc = None
        for part in _swiglu_parts(h, wg_ref, wu_ref, wd_ref):
            acc = part if acc is None else acc + part
            yield
        x3 = x2 + 0.5 * acc
        o_ref[rows, :] = _rms(x3, nf_ref[...])

    step = tm // POST_BLOCKS
    _interleave([block(slice(i, i + step)) for i in range(0, tm, step)], POST_BLOCK_LAG)


def _const_spec(shape):
    nd = len(shape)
    return pl.BlockSpec(shape, lambda *_: (0,) * nd, pipeline_mode=pl.Buffered(1))


def _head_block_ones(n, scale):
    i = jnp.arange(n) // HEAD_SIZE
    return jnp.where(i[:, None] == i[None, :], scale, 0.0).astype(BF16)


def kernel(x, ffn1_norm, ffn1_w_gate, ffn1_w_up, ffn1_w_down, mix_norm, w_in, mu_shift, w0, w_lora_up, a0, a_lora_up, g_lora_up, k_k, k_a, r_k, ln_w, ln_b, w_pool, pool_scale, w_out, ffn2_norm, ffn2_w_gate, ffn2_w_up, ffn2_w_down, final_norm):
    bsz, seq, _ = x.shape
    assert ffn1_norm.shape[0] == 1, "one trunk layer"
    tm = min(TOKEN_TILE, seq)
    tb = min(SCAN_TILE, seq)
    assert seq % tm == 0 and seq % tb == 0 and tb % CHUNK == 0 and tm >= POOL_HALO
    n_chunks = seq // CHUNK
    cpt = tb // CHUNK

    hsum = _head_block_ones(GROUP_LANES, 1.0)
    hmean = _head_block_ones(GROUP_LANES, 1.0 / HEAD_SIZE)
    row = lambda v: v.reshape(1, -1).astype(F32)

    tok = lambda width: pl.BlockSpec((None, tm, width), lambda b, t: (b, t, 0))
    tok_shape = lambda width, dtype=F32: jax.ShapeDtypeStruct((bsz, seq, width), dtype)

    w_in_p = jnp.concatenate(
        [w_in[0][:, :RWKV_COLS], jnp.zeros((D_MODEL, LORA_PAD - LORA_COLS), F32),
         w_in[0][:, RWKV_COLS:]], axis=1)
    mu_p = jnp.concatenate([mu_shift[0], jnp.zeros((LORA_PAD - LORA_COLS,), F32)])

    def lora_rows(w_up, start):
        pad = jnp.zeros((LORA_PAD, RWKV_WIDTH), F32)
        return pad.at[start:start + w_up.shape[0]].set(w_up).astype(BF16)

    l = 0
    if True:
        pre_in = [
            (x, tok(D_MODEL)),
            (row(ffn1_norm[l]), None),
            (ffn1_w_gate[l].astype(BF16), None),
            (ffn1_w_up[l].astype(BF16), None),
            (ffn1_w_down[l].astype(BF16), None),
            (row(mix_norm[l]), None),
            (w_in_p.astype(BF16), None),
            (row(mu_p), None),
            (row(w0[l]), None),
            (lora_rows(w_lora_up[l], 0), None),
            (row(a0[l]), None),
            (lora_rows(a_lora_up[l], D_DECAY_LORA), None),
            (lora_rows(g_lora_up[l], D_DECAY_LORA + D_AAA_LORA), None),
            (row(k_k[l]), None),
            (row(k_a[l]), None),
            (row(r_k[l]), None),
            (w_pool[l].astype(BF16), None),
            (row(pool_scale[l]), None),
            (hsum, None),
        ]
        pre_args = [a for a, _ in pre_in]
        pre_specs = [s if s is not None else _const_spec(a.shape) for a, s in pre_in]
        pre_dtypes = [BF16, BF16, BF16, BF16, BF16, F32, BF16, BF16, BF16]
        x1, r, k, v, kk, b, ld, g, bonus, ypool = pl.pallas_call(
            _pre_kernel,
            grid=(bsz, seq // tm),
            in_specs=pre_specs,
            out_specs=[tok(D_MODEL)] + [tok(RWKV_WIDTH)] * len(pre_dtypes),
            out_shape=[tok_shape(D_MODEL)] + [tok_shape(RWKV_WIDTH, d) for d in pre_dtypes],
            scratch_shapes=[pltpu.VMEM((8, SHIFT_COLS), F32),
                            pltpu.VMEM((POOL_HALO + tm, POOL_WIDTH), F32)],
            compiler_params=pltpu.CompilerParams(
                dimension_semantics=("arbitrary", "arbitrary"), vmem_limit_bytes=VMEM_LIMIT),
            name="pre",
        )(*pre_args)

        grp = pl.BlockSpec((None, tb, GROUP_LANES), lambda bb, gg, tt: (bb, tt, gg))
        mat = pl.BlockSpec((None, cpt, CHUNK, GROUP_LANES), lambda bb, gg, tt: (bb, tt, 0, gg))
        mat_shape = lambda dtype: jax.ShapeDtypeStruct((bsz, n_chunks, CHUNK, RWKV_WIDTH), dtype)
        q, y0, m, n = pl.pallas_call(
            _chunk_kernel,
            grid=(bsz, N_GROUPS, seq // tb),
            in_specs=[grp] * 6,
            out_specs=[grp, grp, mat, mat],
            out_shape=[tok_shape(RWKV_WIDTH, BF16), tok_shape(RWKV_WIDTH), mat_shape(BF16),
                       mat_shape(F32)],
            compiler_params=pltpu.CompilerParams(
                dimension_semantics=("arbitrary", "arbitrary", "arbitrary"),
                vmem_limit_bytes=VMEM_LIMIT),
            name="chunk",
        )(r, k, v, kk, b, ld)

        nb = SCAN_BATCH if bsz % SCAN_BATCH == 0 else 1
        tokb = pl.BlockSpec((nb, tb, RWKV_WIDTH), lambda bb, tt: (bb, tt, 0))
        matb = pl.BlockSpec((nb, cpt, CHUNK, RWKV_WIDTH), lambda bb, tt: (bb, tt, 0, 0))
        y_rw = pl.pallas_call(
            _scan_kernel,
            grid=(bsz // nb, seq // tb),
            in_specs=[tokb, tokb, matb, matb, tokb, tokb,
                      _const_spec((1, RWKV_WIDTH)), _const_spec((1, RWKV_WIDTH)),
                      _const_spec((GROUP_LANES, GROUP_LANES))],
            out_specs=tokb,
            out_shape=tok_shape(RWKV_WIDTH, BF16),
            scratch_shapes=[pltpu.VMEM((nb, HEAD_SIZE, RWKV_WIDTH), F32),
                            pltpu.VMEM((nb, tb, RWKV_WIDTH), F32)],
            compiler_params=pltpu.CompilerParams(
                dimension_semantics=("arbitrary", "arbitrary"), vmem_limit_bytes=VMEM_LIMIT),
            name="scan",
        )(q, y0, m, n, g, bonus, row(ln_w[l]), row(ln_b[l]), hmean)

        post_in = [
            (x1, tok(D_MODEL)),
            (y_rw, tok(RWKV_WIDTH)),
            (ypool, tok(POOL_WIDTH)),
            (w_out[l].astype(BF16), None),
            (row(ffn2_norm[l]), None),
            (ffn2_w_gate[l].astype(BF16), None),
            (ffn2_w_up[l].astype(BF16), None),
            (ffn2_w_down[l].astype(BF16), None),
            (row(final_norm), None),
        ]
        post_args = [a for a, _ in post_in]
        post_specs = [s if s is not None else _const_spec(a.shape) for a, s in post_in]
        x = pl.pallas_call(
            functools.partial(_post_kernel),
            grid=(bsz, seq // tm),
            in_specs=post_specs,
            out_specs=tok(D_MODEL),
            out_shape=tok_shape(D_MODEL),
            compiler_params=pltpu.CompilerParams(
                dimension_semantics=("arbitrary", "arbitrary"), vmem_limit_bytes=VMEM_LIMIT),
            name="post",
        )(*post_args)
    return x
```

```python
import functools

import jax
import jax.numpy as jnp
from jax import lax
from jax.experimental import pallas as pl
from jax.experimental.pallas import tpu as pltpu

F32 = jnp.float32
BF16 = jnp.bfloat16

D_MODEL = 1024
RWKV_WIDTH = 512
POOL_WIDTH = 512
HEAD_SIZE = 64
D_DECAY_LORA = 32
D_AAA_LORA = 32
D_GATE_LORA = 96
LORA_COLS = D_DECAY_LORA + D_AAA_LORA + D_GATE_LORA
LORA_PAD = 256
POOL_WINDOWS = (2, 4, 8, 16)
POOL_GROUP = 128
POOL_HALO = 16
D_FF = 2816
RWKV_COLS = 3 * RWKV_WIDTH + LORA_COLS
SHIFT_COLS = 3 * RWKV_WIDTH + LORA_PAD
RMS_EPS = 1e-6
GN_EPS = 64e-5
L2_EPS = 1e-12

CHUNK = 64
GROUP_LANES = 256
HEADS_PER_GROUP = GROUP_LANES // HEAD_SIZE
N_GROUPS = RWKV_WIDTH // GROUP_LANES
MXU_TILE = 256
FF_CHUNKS = (6 * MXU_TILE, 5 * MXU_TILE)
assert sum(FF_CHUNKS) == D_FF

TOKEN_TILE = 512
PRE_BLOCKS = 4
PRE_BLOCK_LAG = 1
POST_BLOCKS = 1
POST_BLOCK_LAG = 0
SCAN_TILE = 1024
SCAN_BATCH = 2
SCAN_NORM_CHUNKS = 4
CUMSUM_ROWS = 256
CHUNK_GROUPS = 1
CHUNK_GROUP_LAG = 0
VMEM_LIMIT = 56 * 1024 * 1024


def _dot(a, b):
    return jnp.dot(a, b, preferred_element_type=F32)


def _dot_nt(a, b):
    return lax.dot_general(a, b, (((1,), (1,)), ((), ())), preferred_element_type=F32)


def _interleave(gens, lag=0):
    live = list(enumerate(gens))
    rnd = 0
    while live:
        for g, gen in list(live):
            if rnd >= g * lag and next(gen, StopIteration) is StopIteration:
                live.remove((g, gen))
        rnd += 1


def _head_sums(x, ones_bd):
    return _dot(x.astype(BF16), ones_bd)


def _rms(x, g):
    return x * lax.rsqrt(jnp.mean(x * x, axis=-1, keepdims=True) + RMS_EPS) * g


def _swiglu_parts(hb, wg_ref, wu_ref, wd_ref):
    c0 = 0
    for width in FF_CHUNKS:
        gate = _dot(hb, wg_ref[:, c0:c0 + width])
        up = _dot(hb, wu_ref[:, c0:c0 + width])
        act = (gate * jax.nn.sigmoid(gate) * up).astype(BF16)
        yield _dot(act, wd_ref[c0:c0 + width, :])
        c0 += width


def _swiglu(hb, wg_ref, wu_ref, wd_ref):
    return functools.reduce(lambda a, b: a + b, _swiglu_parts(hb, wg_ref, wu_ref, wd_ref))


def _pre_kernel(x_ref, n1_ref, wg_ref, wu_ref, wd_ref, nm_ref, win_ref, mu_ref, w0_ref,
                wlu_ref, a0_ref, alu_ref, glu_ref, kk_ref, ka_ref, rk_ref, wpool_ref,
                pscale_ref, hsum_ref,
                x1_ref, r_ref, k_ref, v_ref, kkn_ref, b_ref, ld_ref, g_ref, bonus_ref,
                ypool_ref, carry_ref, pool_ref):
    tm = x_ref.shape[0]
    t = pl.program_id(1)

    @pl.when(t == 0)
    def _():
        carry_ref[...] = jnp.zeros_like(carry_ref)
        pool_ref[0:POOL_HALO, :] = jnp.zeros((POOL_HALO, POOL_WIDTH), F32)

    def block(rows):
        n = rows.stop - rows.start
        x = x_ref[rows, :]
        h = _rms(x, n1_ref[...]).astype(BF16)
        yield
        acc = None
        for part in _swiglu_parts(h, wg_ref, wu_ref, wd_ref):
            acc = part if acc is None else acc + part
            yield
        x1 = x + 0.5 * acc
        x1_ref[rows, :] = x1
        h2 = _rms(x1, nm_ref[...]).astype(BF16)
        p = _dot(h2, win_ref[...])
        yield

        p_rw = p[:, :SHIFT_COLS]
        row = lax.broadcasted_iota(jnp.int32, (n, SHIFT_COLS), 0)
        prev = jnp.where(row == 0, carry_ref[0:1, :], pltpu.roll(p_rw, 1, 0))
        carry_ref[0:1, :] = p_rw[n - 1:n, :]
        p_rw = p_rw + (prev - p_rw) * mu_ref[...]

        p_r = p_rw[:, 0:RWKV_WIDTH]
        p_k = p_rw[:, RWKV_WIDTH:2 * RWKV_WIDTH]
        p_v = p_rw[:, 2 * RWKV_WIDTH:3 * RWKV_WIDTH]
        p_lora = p_rw[:, 3 * RWKV_WIDTH:SHIFT_COLS]

        hsum = hsum_ref[...]
        z = w0_ref[...] + _dot(jnp.tanh(p_lora).astype(BF16), wlu_ref[...])
        ld_ref[rows, :] = (-jnp.exp(-0.5)) * jax.nn.sigmoid(z)
        a = jax.nn.sigmoid(a0_ref[...] + _dot(p_lora.astype(BF16), alu_ref[...]))
        g_ref[rows, :] = _dot(jax.nn.sigmoid(p_lora).astype(BF16),
                              glu_ref[...]).astype(g_ref.dtype)

        kk = p_k * kk_ref[...]
        k = p_k * (1.0 + (a - 1.0) * ka_ref[...])
        rkr = p_r * k * rk_ref[...]
        r_ref[rows, :] = p_r.astype(r_ref.dtype)
        k_ref[rows, :] = k.astype(k_ref.dtype)
        v_ref[rows, :] = p_v.astype(v_ref.dtype)
        yield
        for gi in range(N_GROUPS):
            gs = slice(gi * GROUP_LANES, (gi + 1) * GROUP_LANES)
            kk_g = kk[:, gs]
            kk_g = kk_g / jnp.maximum(jnp.sqrt(_head_sums(kk_g * kk_g, hsum)), L2_EPS)
            kkn_ref[rows, gs] = kk_g.astype(kkn_ref.dtype)
            b_ref[rows, gs] = (kk_g * a[:, gs]).astype(b_ref.dtype)
            bonus_ref[rows, gs] = (_head_sums(rkr[:, gs], hsum) * p_v[:, gs]).astype(bonus_ref.dtype)
        yield

        base = POOL_HALO + rows.start
        pool_ref[base:base + n, :] = p[:, SHIFT_COLS:]
        pos = t * tm + rows.start + lax.broadcasted_iota(jnp.int32, (n, POOL_GROUP), 0) + 1
        for gi, win in enumerate(POOL_WINDOWS):
            cols = slice(gi * POOL_GROUP, (gi + 1) * POOL_GROUP)
            cur = pool_ref[base:base + n, cols]
            wsum = cur
            for j in range(1, win):
                wsum = wsum + pool_ref[base - j:base - j + n, cols]
            count = jnp.minimum(pos, win).astype(F32)
            pooled = wsum / count - cur
            mixed = _dot(pooled.astype(BF16), wpool_ref[gi])
            ypool_ref[rows, cols] = (mixed * pscale_ref[:, cols]).astype(ypool_ref.dtype)

    step = tm // PRE_BLOCKS
    _interleave([block(slice(i, i + step)) for i in range(0, tm, step)], PRE_BLOCK_LAG)
    pool_ref[0:POOL_HALO, :] = pool_ref[tm:tm + POOL_HALO, :]


def _block_diag(x, bmask):
    return jnp.where(bmask, jnp.concatenate([x] * HEADS_PER_GROUP, axis=0), 0.0)


def _head_transpose(x):
    half = GROUP_LANES // 2
    y = jnp.concatenate([x[:, :half], x[:, half:]], axis=0)
    yt = y.T
    return jnp.concatenate([yt[:HEAD_SIZE], yt[HEAD_SIZE:]], axis=1)


def _perm_head(block):
    return (block % 2) * 2 + block // 2


def _chunk_kernel(r_ref, k_ref, v_ref, kk_ref, b_ref, ld_ref, q_ref, y0_ref, mt_ref, nt_ref):
    tb = r_ref.shape[0]
    gl = GROUP_LANES

    cb = min(tb, CUMSUM_ROWS)
    ri = lax.broadcasted_iota(jnp.int32, (cb, cb), 0)
    ci = lax.broadcasted_iota(jnp.int32, (cb, cb), 1)
    tri = jnp.where((ri // CHUNK == ci // CHUNK) & (ci <= ri), 1.0, 0.0).astype(BF16)
    ld16 = ld_ref[...].astype(BF16)
    ld = ld16.astype(F32)
    cum = jnp.concatenate([_dot(tri, ld16[i:i + cb]) for i in range(0, tb, cb)], axis=0)

    bi = lax.broadcasted_iota(jnp.int32, (gl, gl), 0)
    bj = lax.broadcasted_iota(jnp.int32, (gl, gl), 1)
    bmask = (bi // HEAD_SIZE) == (bj // HEAD_SIZE)
    trow = lax.broadcasted_iota(jnp.int32, (CHUNK, gl), 0)
    scol = lax.broadcasted_iota(jnp.int32, (CHUNK, gl), 1) % CHUNK
    strict = scol < trow
    incl = scol <= trow
    diag_cat = scol == trow
    eye_cat = jnp.where(diag_cat, 1.0, 0.0)
    bdb = lambda x: _block_diag(x, bmask).astype(BF16)
    pmask =_perm_head(bi // HEAD_SIZE) == (bj // HEAD_SIZE)
    bdp = lambda x: jnp.where(pmask, jnp.concatenate([x] * HEADS_PER_GROUP, axis=0), 0.0)
    f32 = lambda ref, sl: ref[sl, :].astype(F32)

    def chain(ids):
        chunks = range(len(ids))
        sls = [slice(c * CHUNK, (c + 1) * CHUNK) for c in ids]
        lc = [cum[sl] for sl in sls]
        l_end = [x[CHUNK - 1:CHUNK, :] for x in lc]
        v = [v_ref[sl, :] for sl in sls]
        b = [f32(b_ref, sl) for sl in sls]
        k = [f32(k_ref, sl) for sl in sls]
        rh = [f32(r_ref, sl) * jnp.exp(lc[c]) for c, sl in enumerate(sls)]
        ah = [-f32(kk_ref, sl) * jnp.exp(lc[c] - ld[sl]) for c, sl in enumerate(sls)]
        p_inv = [jnp.exp(-x) for x in lc]
        p_end = [jnp.exp(l_end[c] - lc[c]) for c in chunks]
        bh = [b[c] * p_inv[c] for c in chunks]
        kh = [k[c] * p_inv[c] for c in chunks]
        b_end = [(b[c] * p_end[c]).astype(BF16) for c in chunks]
        k_end = [(k[c] * p_end[c]).astype(BF16) for c in chunks]
        yield
        v_bd = [bdb(x.astype(F32)) for x in v]
        lhs = [jnp.concatenate([ah[c], rh[c]], axis=0).astype(BF16) for c in chunks]
        gb = [_dot_nt(lhs[c], bdb(bh[c])) for c in chunks]
        gk = [_dot_nt(lhs[c], bdb(kh[c])) for c in chunks]
        a_ab = [jnp.where(strict, x[:CHUNK], 0.0) for x in gb]
        a_rb = [jnp.where(incl, x[CHUNK:], 0.0).astype(BF16) for x in gb]
        a_ak = [jnp.where(strict, x[:CHUNK], 0.0).astype(BF16) for x in gk]
        a_rk = [jnp.where(incl, x[CHUNK:], 0.0).astype(BF16) for x in gk]
        yield

        tinv = [eye_cat + x for x in a_ab]
        pw = [_dot(x.astype(BF16), bdb(x)) for x in a_ab]
        yield
        for _ in range(CHUNK.bit_length() - 3):
            both = [_dot(jnp.concatenate([pw[c], tinv[c]], axis=0).astype(BF16), bdb(pw[c]))
                    for c in chunks]
            tinv = [tinv[c] + both[c][CHUNK:] for c in chunks]
            pw = [x[:CHUNK] for x in both]
            yield
        tinv = [tinv[c] + _dot(tinv[c].astype(BF16), bdb(pw[c])) for c in chunks]
        yield

        bt = [_dot(a_rb[c], bdb(tinv[c])) for c in chunks]
        zk = [_dot(jnp.concatenate([a_ak[c], a_rk[c]], axis=0), v_bd[c]) for c in chunks]
        tbt = [jnp.concatenate([tinv[c], bt[c]], axis=0).astype(BF16) for c in chunks]
        yield
        wq = [_dot(tbt[c], bdb(ah[c])) for c in chunks]
        uy = [_dot(tbt[c], bdb(zk[c][:CHUNK])) for c in chunks]
        w = [x[:CHUNK] for x in wq]
        ut = [x[:CHUNK] for x in uy]
        for c, sl in enumerate(sls):
            q_ref[sl, :] = (rh[c] + wq[c][CHUNK:]).astype(q_ref.dtype)
            y0_ref[sl, :] = uy[c][CHUNK:] + zk[c][CHUNK:]
        yield

        wu_t = [jnp.concatenate([_head_transpose(w[c]), _head_transpose(ut[c])],
                                axis=0).astype(BF16) for c in chunks]
        v_t = [_head_transpose(x.astype(F32)).astype(BF16) for x in v]
        for c in chunks:
            mn = _dot(wu_t[c], bdp(b_end[c]))
            nt_ref[ids[c]] = mn[HEAD_SIZE:] + _dot(v_t[c], bdp(k_end[c]))
            mt_ref[ids[c]] = (jnp.where(diag_cat, jnp.exp(l_end[c]), 0.0)
                              + mn[:HEAD_SIZE]).astype(mt_ref.dtype)

    n_chunks = tb // CHUNK
    per = max(1, n_chunks // CHUNK_GROUPS)
    _interleave([chain(list(range(s, min(s + per, n_chunks)))) for s in range(0, n_chunks, per)],
                CHUNK_GROUP_LAG)


def _scan_kernel(q_ref, y0_ref, mt_ref, nt_ref, g_ref, bonus_ref, lnw_ref, lnb_ref, hmean_ref,
                 y_ref, ht_ref, yraw_ref):
    gl = GROUP_LANES
    nb, tb, _ = q_ref.shape

    @pl.when(pl.program_id(1) == 0)
    def _():
        ht_ref[...] = jnp.zeros_like(ht_ref)

    bi = lax.broadcasted_iota(jnp.int32, (gl, gl), 0)
    bj = lax.broadcasted_iota(jnp.int32, (gl, gl), 1)
    bmask = (bi // HEAD_SIZE) == (bj // HEAD_SIZE)
    bdb = lambda x: _block_diag(x, bmask).astype(BF16)
    seqs = [(b, slice(gi * gl, (gi + 1) * gl)) for b in range(nb) for gi in range(N_GROUPS)]

    def normalize(rows):
        for b, gs in seqs:
            y = yraw_ref[b, rows, gs]
            d = y - _head_sums(y, hmean_ref[...])
            var = _head_sums(d * d, hmean_ref[...])
            yn = d * lax.rsqrt(var + GN_EPS) * lnw_ref[:, gs] + lnb_ref[:, gs]
            y_ref[b, rows, gs] = ((yn + bonus_ref[b, rows, gs])
                                  * g_ref[b, rows, gs]).astype(y_ref.dtype)

    ht = [ht_ref[b, :, gs] for b, gs in seqs]
    n_chunks = tb // CHUNK
    for c in range(n_chunks):
        sl = slice(c * CHUNK, (c + 1) * CHUNK)
        nxt = [_dot(ht[i].astype(BF16), bdb(mt_ref[b, c, :, gs])) + nt_ref[b, c, :, gs]
               for i, (b, gs) in enumerate(seqs)]
        for i, (b, gs) in enumerate(seqs):
            yraw_ref[b, sl, gs] = (_dot_nt(q_ref[b, sl, gs].astype(BF16), bdb(ht[i]))
                                   + y0_ref[b, sl, gs])
        ht = nxt
        if (c + 1) % SCAN_NORM_CHUNKS == 0 or c + 1 == n_chunks:
            first = c + 1 - ((c % SCAN_NORM_CHUNKS) + 1)
            normalize(slice(first * CHUNK, (c + 1) * CHUNK))
    for i, (b, gs) in enumerate(seqs):
        ht_ref[b, :, gs] = ht[i]


def _post_kernel(x1_ref, yrw_ref, ypool_ref, wout_ref, n2_ref, wg_ref, wu_ref, wd_ref, nf_ref,
                 o_ref):
    tm = x1_ref.shape[0]

    def block(rows):
        mix = _dot(yrw_ref[rows, :].astype(BF16), wout_ref[0:RWKV_WIDTH, :])
        mix = mix + _dot(ypool_ref[rows, :].astype(BF16), wout_ref[RWKV_WIDTH:, :])
        x2 = x1_ref[rows, :] + mix
        h = _rms(x2, n2_ref[...]).astype(BF16)
        yield
        acc = None
        for part in _swiglu_parts(h, wg_ref, wu_ref, wd_ref):
            acc = part if acc is None else acc + part
            yield
        x3 = x2 + 0.5 * acc
        o_ref[rows, :] = _rms(x3, nf_ref[...])

    step = tm // POST_BLOCKS
    _interleave([block(slice(i, i + step)) for i in range(0, tm, step)], POST_BLOCK_LAG)


def _const_spec(shape):
    nd = len(shape)
    return pl.BlockSpec(shape, lambda *_: (0,) * nd, pipeline_mode=pl.Buffered(1))


def _head_block_ones(n, scale):
    i = jnp.arange(n) // HEAD_SIZE
    return jnp.where(i[:, None] == i[None, :], scale, 0.0).astype(BF16)


def kernel(x, ffn1_norm, ffn1_w_gate, ffn1_w_up, ffn1_w_down, mix_norm, w_in, mu_shift, w0, w_lora_up, a0, a_lora_up, g_lora_up, k_k, k_a, r_k, ln_w, ln_b, w_pool, pool_scale, w_out, ffn2_norm, ffn2_w_gate, ffn2_w_up, ffn2_w_down, final_norm):
    bsz, seq, _ = x.shape
    assert ffn1_norm.shape[0] == 1, "one trunk layer"
    tm = min(TOKEN_TILE, seq)
    tb = min(SCAN_TILE, seq)
    assert seq % tm == 0 and seq % tb == 0 and tb % CHUNK == 0 and tm >= POOL_HALO
    n_chunks = seq // CHUNK
    cpt = tb // CHUNK

    hsum = _head_block_ones(GROUP_LANES, 1.0)
    hmean = _head_block_ones(GROUP_LANES, 1.0 / HEAD_SIZE)
    row = lambda v: v.reshape(1, -1).astype(F32)

    tok = lambda width: pl.BlockSpec((None, tm, width), lambda b, t: (b, t, 0))
    tok_shape = lambda width, dtype=F32: jax.ShapeDtypeStruct((bsz, seq, width), dtype)

    w_in_p = jnp.concatenate(
        [w_in[0][:, :RWKV_COLS], jnp.zeros((D_MODEL, LORA_PAD - LORA_COLS), F32),
         w_in[0][:, RWKV_COLS:]], axis=1)
    mu_p = jnp.concatenate([mu_shift[0], jnp.zeros((LORA_PAD - LORA_COLS,), F32)])

    def lora_rows(w_up, start):
        pad = jnp.zeros((LORA_PAD, RWKV_WIDTH), F32)
        return pad.at[start:start + w_up.shape[0]].set(w_up).astype(BF16)

    l = 0
    if True:
        pre_in = [
            (x, tok(D_MODEL)),
            (row(ffn1_norm[l]), None),
            (ffn1_w_gate[l].astype(BF16), None),
            (ffn1_w_up[l].astype(BF16), None),
            (ffn1_w_down[l].astype(BF16), None),
            (row(mix_norm[l]), None),
            (w_in_p.astype(BF16), None),
            (row(mu_p), None),
            (row(w0[l]), None),
            (lora_rows(w_lora_up[l], 0), None),
            (row(a0[l]), None),
            (lora_rows(a_lora_up[l], D_DECAY_LORA), None),
            (lora_rows(g_lora_up[l], D_DECAY_LORA + D_AAA_LORA), None),
            (row(k_k[l]), None),
            (row(k_a[l]), None),
            (row(r_k[l]), None),
            (w_pool[l].astype(BF16), None),
            (row(pool_scale[l]), None),
            (hsum, None),
        ]
        pre_args = [a for a, _ in pre_in]
        pre_specs = [s if s is not None else _const_spec(a.shape) for a, s in pre_in]
        pre_dtypes = [BF16, BF16, BF16, BF16, BF16, F32, BF16, BF16, BF16]
        x1, r, k, v, kk, b, ld, g, bonus, ypool = pl.pallas_call(
            _pre_kernel,
            grid=(bsz, seq // tm),
            in_specs=pre_specs,
            out_specs=[tok(D_MODEL)] + [tok(RWKV_WIDTH)] * len(pre_dtypes),
            out_shape=[tok_shape(D_MODEL)] + [tok_shape(RWKV_WIDTH, d) for d in pre_dtypes],
            scratch_shapes=[pltpu.VMEM((8, SHIFT_COLS), F32),
                            pltpu.VMEM((POOL_HALO + tm, POOL_WIDTH), F32)],
            compiler_params=pltpu.CompilerParams(
                dimension_semantics=("arbitrary", "arbitrary"), vmem_limit_bytes=VMEM_LIMIT),
            name="pre",
        )(*pre_args)

        grp = pl.BlockSpec((None, tb, GROUP_LANES), lambda bb, gg, tt: (bb, tt, gg))
        mat = pl.BlockSpec((None, cpt, CHUNK, GROUP_LANES), lambda bb, gg, tt: (bb, tt, 0, gg))
        mat_shape = lambda dtype: jax.ShapeDtypeStruct((bsz, n_chunks, CHUNK, RWKV_WIDTH), dtype)
        q, y0, m, n = pl.pallas_call(
            _chunk_kernel,
            grid=(bsz, N_GROUPS, seq // tb),
            in_specs=[grp] * 6,
            out_specs=[grp, grp, mat, mat],
            out_shape=[tok_shape(RWKV_WIDTH, BF16), tok_shape(RWKV_WIDTH), mat_shape(BF16),
                       mat_shape(F32)],
            compiler_params=pltpu.CompilerParams(
                dimension_semantics=("arbitrary", "arbitrary", "arbitrary"),
                vmem_limit_bytes=VMEM_LIMIT),
            name="chunk",
        )(r, k, v, kk, b, ld)

        nb = SCAN_BATCH if bsz % SCAN_BATCH == 0 else 1
        tokb = pl.BlockSpec((nb, tb, RWKV_WIDTH), lambda bb, tt: (bb, tt, 0))
        matb = pl.BlockSpec((nb, cpt, CHUNK, RWKV_WIDTH), lambda bb, tt: (bb, tt, 0, 0))
        y_rw = pl.pallas_call(
            _scan_kernel,
            grid=(bsz // nb, seq // tb),
            in_specs=[tokb, tokb, matb, matb, tokb, tokb,
                      _const_spec((1, RWKV_WIDTH)), _const_spec((1, RWKV_WIDTH)),
                      _const_spec((GROUP_LANES, GROUP_LANES))],
            out_specs=tokb,
            out_shape=tok_shape(RWKV_WIDTH, BF16),
            scratch_shapes=[pltpu.VMEM((nb, HEAD_SIZE, RWKV_WIDTH), F32),
                            pltpu.VMEM((nb, tb, RWKV_WIDTH), F32)],
            compiler_params=pltpu.CompilerParams(
                dimension_semantics=("arbitrary", "arbitrary"), vmem_limit_bytes=VMEM_LIMIT),
            name="scan",
        )(q, y0, m, n, g, bonus, row(ln_w[l]), row(ln_b[l]), hmean)

        post_in = [
            (x1, tok(D_MODEL)),
            (y_rw, tok(RWKV_WIDTH)),
            (ypool, tok(POOL_WIDTH)),
            (w_out[l].astype(BF16), None),
            (row(ffn2_norm[l]), None),
            (ffn2_w_gate[l].astype(BF16), None),
            (ffn2_w_up[l].astype(BF16), None),
            (ffn2_w_down[l].astype(BF16), None),
            (row(final_norm), None),
        ]
        post_args = [a for a, _ in post_in]
        post_specs = [s if s is not None else _const_spec(a.shape) for a, s in post_in]
        x = pl.pallas_call(
            _post_kernel,
            grid=(bsz, seq // tm),
            in_specs=post_specs,
            out_specs=tok(D_MODEL),
            out_shape=tok_shape(D_MODEL),
            compiler_params=pltpu.CompilerParams(
                dimension_semantics=("arbitrary", "arbitrary"), vmem_limit_bytes=VMEM_LIMIT),
            name="post",
        )(*post_args)
    return x
```

```python
import functools

import jax
import jax.numpy as jnp
from jax import lax
from jax.experimental import pallas as pl
from jax.experimental.pallas import tpu as pltpu

F32 = jnp.float32
BF16 = jnp.bfloat16

D_MODEL = 1024
RWKV_WIDTH = 512
POOL_WIDTH = 512
HEAD_SIZE = 64
D_DECAY_LORA = 32
D_AAA_LORA = 32
D_GATE_LORA = 96
LORA_COLS = D_DECAY_LORA + D_AAA_LORA + D_GATE_LORA
LORA_PAD = 256
POOL_WINDOWS = (2, 4, 8, 16)
POOL_GROUP = 128
POOL_HALO = 16
D_FF = 2816
RWKV_COLS = 3 * RWKV_WIDTH + LORA_COLS
SHIFT_COLS = 3 * RWKV_WIDTH + LORA_PAD
RMS_EPS = 1e-6
GN_EPS = 64e-5
L2_EPS = 1e-12

CHUNK = 64
GROUP_LANES = 256
HEADS_PER_GROUP = GROUP_LANES // HEAD_SIZE
N_GROUPS = RWKV_WIDTH // GROUP_LANES
MXU_TILE = 256
FF_CHUNKS = (6 * MXU_TILE, 5 * MXU_TILE)
assert sum(FF_CHUNKS) == D_FF

TOKEN_TILE = 512
PRE_BLOCKS = 4
PRE_BLOCK_LAG = 1
POST_BLOCKS = 1
POST_BLOCK_LAG = 0
SCAN_TILE = 1024
SCAN_BATCH = 2
SCAN_NORM_CHUNKS = 8
CUMSUM_ROWS = 256
CHUNK_GROUPS = 1
CHUNK_GROUP_LAG = 0
VMEM_LIMIT = 56 * 1024 * 1024


def _dot(a, b):
    return jnp.dot(a, b, preferred_element_type=F32)


def _dot_nt(a, b):
    return lax.dot_general(a, b, (((1,), (1,)), ((), ())), preferred_element_type=F32)


def _interleave(gens, lag=0):
    live = list(enumerate(gens))
    rnd = 0
    while live:
        for g, gen in list(live):
            if rnd >= g * lag and next(gen, StopIteration) is StopIteration:
                live.remove((g, gen))
        rnd += 1


def _head_sums(x, ones_bd):
    return _dot(x.astype(BF16), ones_bd)


def _rms(x, g):
    return x * lax.rsqrt(jnp.mean(x * x, axis=-1, keepdims=True) + RMS_EPS) * g


def _swiglu_parts(hb, wg_ref, wu_ref, wd_ref):
    c0 = 0
    for width in FF_CHUNKS:
        gate = _dot(hb, wg_ref[:, c0:c0 + width])
        up = _dot(hb, wu_ref[:, c0:c0 + width])
        act = (gate * jax.nn.sigmoid(gate) * up).astype(BF16)
        yield _dot(act, wd_ref[c0:c0 + width, :])
        c0 += width


def _pre_kernel(x_ref, n1_ref, wg_ref, wu_ref, wd_ref, nm_ref, win_ref, mu_ref, w0_ref,
                wlu_ref, a0_ref, alu_ref, glu_ref, kk_ref, ka_ref, rk_ref, wpool_ref,
                pscale_ref, hsum_ref,
                x1_ref, r_ref, k_ref, v_ref, kkn_ref, b_ref, ld_ref, g_ref, bonus_ref,
                ypool_ref, carry_ref, pool_ref):
    tm = x_ref.shape[0]
    t = pl.program_id(1)

    @pl.when(t == 0)
    def _():
        carry_ref[...] = jnp.zeros_like(carry_ref)
        pool_ref[0:POOL_HALO, :] = jnp.zeros((POOL_HALO, POOL_WIDTH), F32)

    def block(rows):
        n = rows.stop - rows.start
        x = x_ref[rows, :]
        h = _rms(x, n1_ref[...]).astype(BF16)
        yield
        acc = None
        for part in _swiglu_parts(h, wg_ref, wu_ref, wd_ref):
            acc = part if acc is None else acc + part
            yield
        x1 = x + 0.5 * acc
        x1_ref[rows, :] = x1
        h2 = _rms(x1, nm_ref[...]).astype(BF16)
        p = _dot(h2, win_ref[...])
        yield

        p_rw = p[:, :SHIFT_COLS]
        row = lax.broadcasted_iota(jnp.int32, (n, SHIFT_COLS), 0)
        prev = jnp.where(row == 0, carry_ref[0:1, :], pltpu.roll(p_rw, 1, 0))
        carry_ref[0:1, :] = p_rw[n - 1:n, :]
        p_rw = p_rw + (prev - p_rw) * mu_ref[...]

        p_r = p_rw[:, 0:RWKV_WIDTH]
        p_k = p_rw[:, RWKV_WIDTH:2 * RWKV_WIDTH]
        p_v = p_rw[:, 2 * RWKV_WIDTH:3 * RWKV_WIDTH]
        p_lora = p_rw[:, 3 * RWKV_WIDTH:SHIFT_COLS]

        hsum = hsum_ref[...]
        z = w0_ref[...] + _dot(jnp.tanh(p_lora).astype(BF16), wlu_ref[...])
        ld_ref[rows, :] = (-jnp.exp(-0.5)) * jax.nn.sigmoid(z)
        a = jax.nn.sigmoid(a0_ref[...] + _dot(p_lora.astype(BF16), alu_ref[...]))
        g_ref[rows, :] = _dot(jax.nn.sigmoid(p_lora).astype(BF16),
                              glu_ref[...]).astype(g_ref.dtype)

        kk = p_k * kk_ref[...]
        k = p_k * (1.0 + (a - 1.0) * ka_ref[...])
        rkr = p_r * k * rk_ref[...]
        r_ref[rows, :] = p_r.astype(r_ref.dtype)
        k_ref[rows, :] = k.astype(k_ref.dtype)
        v_ref[rows, :] = p_v.astype(v_ref.dtype)
        yield
        groups = [slice(gi * GROUP_LANES, (gi + 1) * GROUP_LANES) for gi in range(N_GROUPS)]
        sums = _head_sums(jnp.concatenate([kk[:, gs] * kk[:, gs] for gs in groups]
                                          + [rkr[:, gs] for gs in groups], axis=0), hsum)
        for gi, gs in enumerate(groups):
            ss = sums[gi * n:(gi + 1) * n]
            rk_sum = sums[(N_GROUPS + gi) * n:(N_GROUPS + gi + 1) * n]
            kk_g = kk[:, gs] / jnp.maximum(jnp.sqrt(ss), L2_EPS)
            kkn_ref[rows, gs] = kk_g.astype(kkn_ref.dtype)
            b_ref[rows, gs] = (kk_g * a[:, gs]).astype(b_ref.dtype)
            bonus_ref[rows, gs] = (rk_sum * p_v[:, gs]).astype(bonus_ref.dtype)
        yield

        base = POOL_HALO + rows.start
        pool_ref[base:base + n, :] = p[:, SHIFT_COLS:]
        pos = t * tm + rows.start + lax.broadcasted_iota(jnp.int32, (n, POOL_GROUP), 0) + 1
        for gi, win in enumerate(POOL_WINDOWS):
            cols = slice(gi * POOL_GROUP, (gi + 1) * POOL_GROUP)
            cur = pool_ref[base:base + n, cols]
            wsum = cur
            for j in range(1, win):
                wsum = wsum + pool_ref[base - j:base - j + n, cols]
            count = jnp.minimum(pos, win).astype(F32)
            pooled = wsum / count - cur
            mixed = _dot(pooled.astype(BF16), wpool_ref[gi])
            ypool_ref[rows, cols] = (mixed * pscale_ref[:, cols]).astype(ypool_ref.dtype)

    step = tm // PRE_BLOCKS
    _interleave([block(slice(i, i + step)) for i in range(0, tm, step)], PRE_BLOCK_LAG)
    pool_ref[0:POOL_HALO, :] = pool_ref[tm:tm + POOL_HALO, :]


def _block_diag(x, bmask):
    return jnp.where(bmask, jnp.concatenate([x] * HEADS_PER_GROUP, axis=0), 0.0)


def _head_transpose(x):
    half = GROUP_LANES // 2
    y = jnp.concatenate([x[:, :half], x[:, half:]], axis=0)
    yt = y.T
    return jnp.concatenate([yt[:HEAD_SIZE], yt[HEAD_SIZE:]], axis=1)


def _perm_head(block):
    return (block % 2) * 2 + block // 2


def _chunk_kernel(r_ref, k_ref, v_ref, kk_ref, b_ref, ld_ref, qt_ref, y0_ref, mt_ref, nt_ref):
    tb = r_ref.shape[0]
    gl = GROUP_LANES

    cb = min(tb, CUMSUM_ROWS)
    ri = lax.broadcasted_iota(jnp.int32, (cb, cb), 0)
    ci = lax.broadcasted_iota(jnp.int32, (cb, cb), 1)
    tri = jnp.where((ri // CHUNK == ci // CHUNK) & (ci <= ri), 1.0, 0.0).astype(BF16)
    ld16 = ld_ref[...].astype(BF16)
    ld = ld16.astype(F32)
    cum = jnp.concatenate([_dot(tri, ld16[i:i + cb]) for i in range(0, tb, cb)], axis=0)

    bi = lax.broadcasted_iota(jnp.int32, (gl, gl), 0)
    bj = lax.broadcasted_iota(jnp.int32, (gl, gl), 1)
    bmask = (bi // HEAD_SIZE) == (bj // HEAD_SIZE)
    trow = lax.broadcasted_iota(jnp.int32, (CHUNK, gl), 0)
    scol = lax.broadcasted_iota(jnp.int32, (CHUNK, gl), 1) % CHUNK
    strict = scol < trow
    incl = scol <= trow
    diag_cat = scol == trow
    eye_cat = jnp.where(diag_cat, 1.0, 0.0)
    bdb = lambda x: _block_diag(x, bmask).astype(BF16)
    pmask =_perm_head(bi // HEAD_SIZE) == (bj // HEAD_SIZE)
    bdp = lambda x: jnp.where(pmask, jnp.concatenate([x] * HEADS_PER_GROUP, axis=0), 0.0)
    f32 = lambda ref, sl: ref[sl, :].astype(F32)

    def chain(ids):
        chunks = range(len(ids))
        sls = [slice(c * CHUNK, (c + 1) * CHUNK) for c in ids]
        lc = [cum[sl] for sl in sls]
        l_end = [x[CHUNK - 1:CHUNK, :] for x in lc]
        v = [v_ref[sl, :] for sl in sls]
        b = [f32(b_ref, sl) for sl in sls]
        k = [f32(k_ref, sl) for sl in sls]
        rh = [f32(r_ref, sl) * jnp.exp(lc[c]) for c, sl in enumerate(sls)]
        ah = [-f32(kk_ref, sl) * jnp.exp(lc[c] - ld[sl]) for c, sl in enumerate(sls)]
        p_inv = [jnp.exp(-x) for x in lc]
        p_end = [jnp.exp(l_end[c] - lc[c]) for c in chunks]
        bh = [b[c] * p_inv[c] for c in chunks]
        kh = [k[c] * p_inv[c] for c in chunks]
        b_end = [(b[c] * p_end[c]).astype(BF16) for c in chunks]
        k_end = [(k[c] * p_end[c]).astype(BF16) for c in chunks]
        yield
        v_bd = [bdb(x.astype(F32)) for x in v]
        lhs = [jnp.concatenate([ah[c], rh[c]], axis=0).astype(BF16) for c in chunks]
        gb = [_dot_nt(lhs[c], bdb(bh[c])) for c in chunks]
        gk = [_dot_nt(lhs[c], bdb(kh[c])) for c in chunks]
        a_ab = [jnp.where(strict, x[:CHUNK], 0.0) for x in gb]
        a_rb = [jnp.where(incl, x[CHUNK:], 0.0).astype(BF16) for x in gb]
        a_ak = [jnp.where(strict, x[:CHUNK], 0.0).astype(BF16) for x in gk]
        a_rk = [jnp.where(incl, x[CHUNK:], 0.0).astype(BF16) for x in gk]
        yield

        tinv = [eye_cat + x for x in a_ab]
        pw = [_dot(x.astype(BF16), bdb(x)) for x in a_ab]
        yield
        for _ in range(CHUNK.bit_length() - 3):
            both = [_dot(jnp.concatenate([pw[c], tinv[c]], axis=0).astype(BF16), bdb(pw[c]))
                    for c in chunks]
            tinv = [tinv[c] + both[c][CHUNK:] for c in chunks]
            pw = [x[:CHUNK] for x in both]
            yield
        tinv = [tinv[c] + _dot(tinv[c].astype(BF16), bdb(pw[c])) for c in chunks]
        yield

        bt = [_dot(a_rb[c], bdb(tinv[c])) for c in chunks]
        zk = [_dot(jnp.concatenate([a_ak[c], a_rk[c]], axis=0), v_bd[c]) for c in chunks]
        tbt = [jnp.concatenate([tinv[c], bt[c]], axis=0).astype(BF16) for c in chunks]
        yield
        wq = [_dot(tbt[c], bdb(ah[c])) for c in chunks]
        uy = [_dot(tbt[c], bdb(zk[c][:CHUNK])) for c in chunks]
        w = [x[:CHUNK] for x in wq]
        ut = [x[:CHUNK] for x in uy]
        for c, sl in enumerate(sls):
            qt_ref[ids[c]] = _head_transpose(rh[c] + wq[c][CHUNK:]).astype(qt_ref.dtype)
            y0_ref[sl, :] = uy[c][CHUNK:] + zk[c][CHUNK:]
        yield

        wu_t = [jnp.concatenate([_head_transpose(w[c]), _head_transpose(ut[c])],
                                axis=0).astype(BF16) for c in chunks]
        v_t = [_head_transpose(x.astype(F32)).astype(BF16) for x in v]
        for c in chunks:
            mn = _dot(wu_t[c], bdp(b_end[c]))
            nt_ref[ids[c]] = mn[HEAD_SIZE:] + _dot(v_t[c], bdp(k_end[c]))
            mt_ref[ids[c]] = (jnp.where(diag_cat, jnp.exp(l_end[c]), 0.0)
                              + mn[:HEAD_SIZE]).astype(mt_ref.dtype)

    n_chunks = tb // CHUNK
    per = max(1, n_chunks // CHUNK_GROUPS)
    _interleave([chain(list(range(s, min(s + per, n_chunks)))) for s in range(0, n_chunks, per)],
                CHUNK_GROUP_LAG)


def _scan_kernel(qt_ref, y0_ref, mt_ref, nt_ref, g_ref, bonus_ref, lnw_ref, lnb_ref, hmean_ref,
                 y_ref, ht_ref, yraw_ref):
    gl = GROUP_LANES
    nb, tb, _ = y0_ref.shape

    @pl.when(pl.program_id(1) == 0)
    def _():
        ht_ref[...] = jnp.zeros_like(ht_ref)

    bi = lax.broadcasted_iota(jnp.int32, (gl, gl), 0)
    bj = lax.broadcasted_iota(jnp.int32, (gl, gl), 1)
    bmask = (bi // HEAD_SIZE) == (bj // HEAD_SIZE)
    pmask = _perm_head(bi // HEAD_SIZE) == (bj // HEAD_SIZE)
    tile4 = lambda x: jnp.concatenate([x] * HEADS_PER_GROUP, axis=0)
    seqs = [(b, slice(gi * gl, (gi + 1) * gl)) for b in range(nb) for gi in range(N_GROUPS)]

    def normalize(rows):
        n = rows.stop - rows.start
        y = jnp.concatenate([yraw_ref[b, rows, gs] for b, gs in seqs], axis=0)
        d = y - _head_sums(y, hmean_ref[...])
        yn = d * lax.rsqrt(_head_sums(d * d, hmean_ref[...]) + GN_EPS)
        for i, (b, gs) in enumerate(seqs):
            out = yn[i * n:(i + 1) * n] * lnw_ref[:, gs] + lnb_ref[:, gs]
            y_ref[b, rows, gs] = ((out + bonus_ref[b, rows, gs])
                                  * g_ref[b, rows, gs]).astype(y_ref.dtype)

    ht = [ht_ref[b, :, gs] for b, gs in seqs]
    n_chunks = tb // CHUNK
    for c in range(n_chunks):
        sl = slice(c * CHUNK, (c + 1) * CHUNK)
        both = [_dot(ht[i].astype(BF16), jnp.concatenate(
            [jnp.where(bmask, tile4(mt_ref[b, c, :, gs]), 0.0),
             jnp.where(pmask, tile4(qt_ref[b, c, :, gs]), 0.0)], axis=1))
            for i, (b, gs) in enumerate(seqs)]
        for i, (b, gs) in enumerate(seqs):
            yraw_ref[b, sl, gs] = _head_transpose(both[i][:, gl:]) + y0_ref[b, sl, gs]
        ht = [both[i][:, :gl] + nt_ref[b, c, :, gs] for i, (b, gs) in enumerate(seqs)]
        if (c + 1) % SCAN_NORM_CHUNKS == 0 or c + 1 == n_chunks:
            first = c + 1 - ((c % SCAN_NORM_CHUNKS) + 1)
            normalize(slice(first * CHUNK, (c + 1) * CHUNK))
    for i, (b, gs) in enumerate(seqs):
        ht_ref[b, :, gs] = ht[i]


def _post_kernel(x1_ref, yrw_ref, ypool_ref, wout_ref, n2_ref, wg_ref, wu_ref, wd_ref, nf_ref,
                 o_ref):
    tm = x1_ref.shape[0]

    def block(rows):
        mix = _dot(yrw_ref[rows, :].astype(BF16), wout_ref[0:RWKV_WIDTH, :])
        mix = mix + _dot(ypool_ref[rows, :].astype(BF16), wout_ref[RWKV_WIDTH:, :])
        x2 = x1_ref[rows, :] + mix
        h = _rms(x2, n2_ref[...]).astype(BF16)
        yield
        acc = None
        for part in _swiglu_parts(h, wg_ref, wu_ref, wd_ref):
            acc = part if acc is None else acc + part
            yield
        x3 = x2 + 0.5 * acc
        o_ref[rows, :] = _rms(x3, nf_ref[...])

    step = tm // POST_BLOCKS
    _interleave([block(slice(i, i + step)) for i in range(0, tm, step)], POST_BLOCK_LAG)


def _const_spec(shape):
    nd = len(shape)
    return pl.BlockSpec(shape, lambda *_: (0,) * nd, pipeline_mode=pl.Buffered(1))


def _head_block_ones(n, scale):
    i = jnp.arange(n) // HEAD_SIZE
    return jnp.where(i[:, None] == i[None, :], scale, 0.0).astype(BF16)


def kernel(x, ffn1_norm, ffn1_w_gate, ffn1_w_up, ffn1_w_down, mix_norm, w_in, mu_shift, w0, w_lora_up, a0, a_lora_up, g_lora_up, k_k, k_a, r_k, ln_w, ln_b, w_pool, pool_scale, w_out, ffn2_norm, ffn2_w_gate, ffn2_w_up, ffn2_w_down, final_norm):
    bsz, seq, _ = x.shape
    assert ffn1_norm.shape[0] == 1, "one trunk layer"
    tm = min(TOKEN_TILE, seq)
    tb = min(SCAN_TILE, seq)
    assert seq % tm == 0 and seq % tb == 0 and tb % CHUNK == 0 and tm >= POOL_HALO
    n_chunks = seq // CHUNK
    cpt = tb // CHUNK

    hsum = _head_block_ones(GROUP_LANES, 1.0)
    hmean = _head_block_ones(GROUP_LANES, 1.0 / HEAD_SIZE)
    row = lambda v: v.reshape(1, -1).astype(F32)

    tok = lambda width: pl.BlockSpec((None, tm, width), lambda b, t: (b, t, 0))
    tok_shape = lambda width, dtype=F32: jax.ShapeDtypeStruct((bsz, seq, width), dtype)

    w_in_p = jnp.concatenate(
        [w_in[0][:, :RWKV_COLS], jnp.zeros((D_MODEL, LORA_PAD - LORA_COLS), F32),
         w_in[0][:, RWKV_COLS:]], axis=1)
    mu_p = jnp.concatenate([mu_shift[0], jnp.zeros((LORA_PAD - LORA_COLS,), F32)])

    def lora_rows(w_up, start):
        pad = jnp.zeros((LORA_PAD, RWKV_WIDTH), F32)
        return pad.at[start:start + w_up.shape[0]].set(w_up).astype(BF16)

    l = 0
    if True:
        pre_in = [
            (x, tok(D_MODEL)),
            (row(ffn1_norm[l]), None),
            (ffn1_w_gate[l].astype(BF16), None),
            (ffn1_w_up[l].astype(BF16), None),
            (ffn1_w_down[l].astype(BF16), None),
            (row(mix_norm[l]), None),
            (w_in_p.astype(BF16), None),
            (row(mu_p), None),
            (row(w0[l]), None),
            (lora_rows(w_lora_up[l], 0), None),
            (row(a0[l]), None),
            (lora_rows(a_lora_up[l], D_DECAY_LORA), None),
            (lora_rows(g_lora_up[l], D_DECAY_LORA + D_AAA_LORA), None),
            (row(k_k[l]), None),
            (row(k_a[l]), None),
            (row(r_k[l]), None),
            (w_pool[l].astype(BF16), None),
            (row(pool_scale[l]), None),
            (hsum, None),
        ]
        pre_args = [a for a, _ in pre_in]
        pre_specs = [s if s is not None else _const_spec(a.shape) for a, s in pre_in]
        pre_dtypes = [BF16, BF16, BF16, BF16, BF16, F32, BF16, BF16, BF16]
        x1, r, k, v, kk, b, ld, g, bonus, ypool = pl.pallas_call(
            _pre_kernel,
            grid=(bsz, seq // tm),
            in_specs=pre_specs,
            out_specs=[tok(D_MODEL)] + [tok(RWKV_WIDTH)] * len(pre_dtypes),
            out_shape=[tok_shape(D_MODEL)] + [tok_shape(RWKV_WIDTH, d) for d in pre_dtypes],
            scratch_shapes=[pltpu.VMEM((8, SHIFT_COLS), F32),
                            pltpu.VMEM((POOL_HALO + tm, POOL_WIDTH), F32)],
            compiler_params=pltpu.CompilerParams(
                dimension_semantics=("arbitrary", "arbitrary"), vmem_limit_bytes=VMEM_LIMIT),
            name="pre",
        )(*pre_args)

        grp = pl.BlockSpec((None, tb, GROUP_LANES), lambda bb, gg, tt: (bb, tt, gg))
        mat = pl.BlockSpec((None, cpt, CHUNK, GROUP_LANES), lambda bb, gg, tt: (bb, tt, 0, gg))
        mat_shape = lambda dtype: jax.ShapeDtypeStruct((bsz, n_chunks, CHUNK, RWKV_WIDTH), dtype)
        q, y0, m, n = pl.pallas_call(
            _chunk_kernel,
            grid=(bsz, N_GROUPS, seq // tb),
            in_specs=[grp] * 6,
            out_specs=[mat, grp, mat, mat],
            out_shape=[mat_shape(BF16), tok_shape(RWKV_WIDTH), mat_shape(BF16), mat_shape(F32)],
            compiler_params=pltpu.CompilerParams(
                dimension_semantics=("arbitrary", "arbitrary", "arbitrary"),
                vmem_limit_bytes=VMEM_LIMIT),
            name="chunk",
        )(r, k, v, kk, b, ld)

        nb = SCAN_BATCH if bsz % SCAN_BATCH == 0 else 1
        tokb = pl.BlockSpec((nb, tb, RWKV_WIDTH), lambda bb, tt: (bb, tt, 0))
        matb = pl.BlockSpec((nb, cpt, CHUNK, RWKV_WIDTH), lambda bb, tt: (bb, tt, 0, 0))
        y_rw = pl.pallas_call(
            _scan_kernel,
            grid=(bsz // nb, seq // tb),
            in_specs=[matb, tokb, matb, matb, tokb, tokb,
                      _const_spec((1, RWKV_WIDTH)), _const_spec((1, RWKV_WIDTH)),
                      _const_spec((GROUP_LANES, GROUP_LANES))],
            out_specs=tokb,
            out_shape=tok_shape(RWKV_WIDTH, BF16),
            scratch_shapes=[pltpu.VMEM((nb, HEAD_SIZE, RWKV_WIDTH), F32),
                            pltpu.VMEM((nb, tb, RWKV_WIDTH), F32)],
            compiler_params=pltpu.CompilerParams(
                dimension_semantics=("arbitrary", "arbitrary"), vmem_limit_bytes=VMEM_LIMIT),
            name="scan",
        )(q, y0, m, n, g, bonus, row(ln_w[l]), row(ln_b[l]), hmean)

        post_in = [
            (x1, tok(D_MODEL)),
            (y_rw, tok(RWKV_WIDTH)),
            (ypool, tok(POOL_WIDTH)),
            (w_out[l].astype(BF16), None),
            (row(ffn2_norm[l]), None),
            (ffn2_w_gate[l].astype(BF16), None),
            (ffn2_w_up[l].astype(BF16), None),
            (ffn2_w_down[l].astype(BF16), None),
            (row(final_norm), None),
        ]
        post_args = [a for a, _ in post_in]
        post_specs = [s if s is not None else _const_spec(a.shape) for a, s in post_in]
        x = pl.pallas_call(
            _post_kernel,
            grid=(bsz, seq // tm),
            in_specs=post_specs,
            out_specs=tok(D_MODEL),
            out_shape=tok_shape(D_MODEL),
            compiler_params=pltpu.CompilerParams(
                dimension_semantics=("arbitrary", "arbitrary"), vmem_limit_bytes=VMEM_LIMIT),
            name="post",
        )(*post_args)
    return x
```

```python
import functools

import jax
import jax.numpy as jnp
from jax import lax
from jax.experimental import pallas as pl
from jax.experimental.pallas import tpu as pltpu

F32 = jnp.float32
BF16 = jnp.bfloat16

D_MODEL = 1024
RWKV_WIDTH = 512
POOL_WIDTH = 512
HEAD_SIZE = 64
D_DECAY_LORA = 32
D_AAA_LORA = 32
D_GATE_LORA = 96
LORA_COLS = D_DECAY_LORA + D_AAA_LORA + D_GATE_LORA
LORA_PAD = 256
POOL_WINDOWS = (2, 4, 8, 16)
POOL_GROUP = 128
POOL_HALO = 16
D_FF = 2816
RWKV_COLS = 3 * RWKV_WIDTH + LORA_COLS
SHIFT_COLS = 3 * RWKV_WIDTH + LORA_PAD
RMS_EPS = 1e-6
GN_EPS = 64e-5
L2_EPS = 1e-12

CHUNK = 64
GROUP_LANES = 256
HEADS_PER_GROUP = GROUP_LANES // HEAD_SIZE
N_GROUPS = RWKV_WIDTH // GROUP_LANES
MXU_TILE = 256
FF_CHUNKS = (6 * MXU_TILE, 5 * MXU_TILE)
assert sum(FF_CHUNKS) == D_FF

TOKEN_TILE = 512
PRE_BLOCKS = 4
PRE_BLOCK_LAG = 1
SCAN_TILE = 1024
SCAN_BATCH = 2
SCAN_NORM_CHUNKS = 8
CUMSUM_ROWS = 256
VMEM_LIMIT = 56 * 1024 * 1024


def _dot(a, b):
    return jnp.dot(a, b, preferred_element_type=F32)


def _dot_nt(a, b):
    return lax.dot_general(a, b, (((1,), (1,)), ((), ())), preferred_element_type=F32)


def _interleave(gens, lag=0):
    live = list(enumerate(gens))
    rnd = 0
    while live:
        for g, gen in list(live):
            if rnd >= g * lag and next(gen, StopIteration) is StopIteration:
                live.remove((g, gen))
        rnd += 1


def _head_sums(x, ones_bd):
    return _dot(x.astype(BF16), ones_bd)


def _rms(x, g):
    return x * lax.rsqrt(jnp.mean(x * x, axis=-1, keepdims=True) + RMS_EPS) * g


def _swiglu_parts(hb, wg_ref, wu_ref, wd_ref):
    c0 = 0
    for width in FF_CHUNKS:
        gate = _dot(hb, wg_ref[:, c0:c0 + width])
        up = _dot(hb, wu_ref[:, c0:c0 + width])
        act = (gate * jax.nn.sigmoid(gate) * up).astype(BF16)
        yield _dot(act, wd_ref[c0:c0 + width, :])
        c0 += width


def _pre_kernel(x_ref, n1_ref, wg_ref, wu_ref, wd_ref, nm_ref, win_ref, mu_ref, w0_ref,
                wlu_ref, a0_ref, alu_ref, glu_ref, kk_ref, ka_ref, rk_ref, wpool_ref,
                pscale_ref, hsum_ref,
                x1_ref, r_ref, k_ref, v_ref, kkn_ref, b_ref, ld_ref, g_ref, bonus_ref,
                ypool_ref, carry_ref, pool_ref):
    tm = x_ref.shape[0]
    t = pl.program_id(1)

    @pl.when(t == 0)
    def _():
        carry_ref[...] = jnp.zeros_like(carry_ref)
        pool_ref[0:POOL_HALO, :] = jnp.zeros((POOL_HALO, POOL_WIDTH), F32)

    def block(rows):
        n = rows.stop - rows.start
        x = x_ref[rows, :]
        h = _rms(x, n1_ref[...]).astype(BF16)
        yield
        acc = None
        for part in _swiglu_parts(h, wg_ref, wu_ref, wd_ref):
            acc = part if acc is None else acc + part
            yield
        x1 = x + 0.5 * acc
        x1_ref[rows, :] = x1
        h2 = _rms(x1, nm_ref[...]).astype(BF16)
        p = _dot(h2, win_ref[...])
        yield

        p_rw = p[:, :SHIFT_COLS]
        row = lax.broadcasted_iota(jnp.int32, (n, SHIFT_COLS), 0)
        prev = jnp.where(row == 0, carry_ref[0:1, :], pltpu.roll(p_rw, 1, 0))
        carry_ref[0:1, :] = p_rw[n - 1:n, :]
        p_rw = p_rw + (prev - p_rw) * mu_ref[...]

        p_r = p_rw[:, 0:RWKV_WIDTH]
        p_k = p_rw[:, RWKV_WIDTH:2 * RWKV_WIDTH]
        p_v = p_rw[:, 2 * RWKV_WIDTH:3 * RWKV_WIDTH]
        p_lora = p_rw[:, 3 * RWKV_WIDTH:SHIFT_COLS]

        hsum = hsum_ref[...]
        z = w0_ref[...] + _dot(jnp.tanh(p_lora).astype(BF16), wlu_ref[...])
        ld_ref[rows, :] = (-jnp.exp(-0.5)) * jax.nn.sigmoid(z)
        a = jax.nn.sigmoid(a0_ref[...] + _dot(p_lora.astype(BF16), alu_ref[...]))
        g_ref[rows, :] = _dot(jax.nn.sigmoid(p_lora).astype(BF16),
                              glu_ref[...]).astype(g_ref.dtype)

        kk = p_k * kk_ref[...]
        k = p_k * (1.0 + (a - 1.0) * ka_ref[...])
        rkr = p_r * k * rk_ref[...]
        r_ref[rows, :] = p_r.astype(r_ref.dtype)
        k_ref[rows, :] = k.astype(k_ref.dtype)
        v_ref[rows, :] = p_v.astype(v_ref.dtype)
        yield
        groups = [slice(gi * GROUP_LANES, (gi + 1) * GROUP_LANES) for gi in range(N_GROUPS)]
        sums = _head_sums(jnp.concatenate([kk[:, gs] * kk[:, gs] for gs in groups]
                                          + [rkr[:, gs] for gs in groups], axis=0), hsum)
        for gi, gs in enumerate(groups):
            ss = sums[gi * n:(gi + 1) * n]
            rk_sum = sums[(N_GROUPS + gi) * n:(N_GROUPS + gi + 1) * n]
            kk_g = kk[:, gs] / jnp.maximum(jnp.sqrt(ss), L2_EPS)
            kkn_ref[rows, gs] = kk_g.astype(kkn_ref.dtype)
            b_ref[rows, gs] = (kk_g * a[:, gs]).astype(b_ref.dtype)
            bonus_ref[rows, gs] = (rk_sum * p_v[:, gs]).astype(bonus_ref.dtype)
        yield

        base = POOL_HALO + rows.start
        pool_ref[base:base + n, :] = p[:, SHIFT_COLS:]
        pos = t * tm + rows.start + lax.broadcasted_iota(jnp.int32, (n, POOL_GROUP), 0) + 1
        for gi, win in enumerate(POOL_WINDOWS):
            cols = slice(gi * POOL_GROUP, (gi + 1) * POOL_GROUP)
            cur = pool_ref[base:base + n, cols]
            wsum = cur
            for j in range(1, win):
                wsum = wsum + pool_ref[base - j:base - j + n, cols]
            count = jnp.minimum(pos, win).astype(F32)
            pooled = wsum / count - cur
            mixed = _dot(pooled.astype(BF16), wpool_ref[gi])
            ypool_ref[rows, cols] = (mixed * pscale_ref[:, cols]).astype(ypool_ref.dtype)

    step = tm // PRE_BLOCKS
    _interleave([block(slice(i, i + step)) for i in range(0, tm, step)], PRE_BLOCK_LAG)
    pool_ref[0:POOL_HALO, :] = pool_ref[tm:tm + POOL_HALO, :]


def _block_diag(x, bmask):
    return jnp.where(bmask, jnp.concatenate([x] * HEADS_PER_GROUP, axis=0), 0.0)


def _head_transpose(x):
    half = GROUP_LANES // 2
    y = jnp.concatenate([x[:, :half], x[:, half:]], axis=0)
    yt = y.T
    return jnp.concatenate([yt[:HEAD_SIZE], yt[HEAD_SIZE:]], axis=1)


def _perm_head(block):
    return (block % 2) * 2 + block // 2


def _chunk_kernel(r_ref, k_ref, v_ref, kk_ref, b_ref, ld_ref, qt_ref, y0_ref, mt_ref, nt_ref):
    tb = r_ref.shape[0]
    gl = GROUP_LANES

    cb = min(tb, CUMSUM_ROWS)
    ri = lax.broadcasted_iota(jnp.int32, (cb, cb), 0)
    ci = lax.broadcasted_iota(jnp.int32, (cb, cb), 1)
    tri = jnp.where((ri // CHUNK == ci // CHUNK) & (ci <= ri), 1.0, 0.0).astype(BF16)
    ld16 = ld_ref[...].astype(BF16)
    ld = ld16.astype(F32)
    cum = jnp.concatenate([_dot(tri, ld16[i:i + cb]) for i in range(0, tb, cb)], axis=0)

    bi = lax.broadcasted_iota(jnp.int32, (gl, gl), 0)
    bj = lax.broadcasted_iota(jnp.int32, (gl, gl), 1)
    bmask = (bi // HEAD_SIZE) == (bj // HEAD_SIZE)
    trow = lax.broadcasted_iota(jnp.int32, (CHUNK, gl), 0)
    scol = lax.broadcasted_iota(jnp.int32, (CHUNK, gl), 1) % CHUNK
    strict = scol < trow
    incl = scol <= trow
    diag_cat = scol == trow
    eye_cat = jnp.where(diag_cat, 1.0, 0.0)
    bdb = lambda x: _block_diag(x, bmask).astype(BF16)
    pmask =_perm_head(bi // HEAD_SIZE) == (bj // HEAD_SIZE)
    bdp = lambda x: jnp.where(pmask, jnp.concatenate([x] * HEADS_PER_GROUP, axis=0), 0.0)
    f32 = lambda ref, sl: ref[sl, :].astype(F32)

    def chain(ids):
        chunks = range(len(ids))
        sls = [slice(c * CHUNK, (c + 1) * CHUNK) for c in ids]
        lc = [cum[sl] for sl in sls]
        l_end = [x[CHUNK - 1:CHUNK, :] for x in lc]
        v = [v_ref[sl, :] for sl in sls]
        b = [f32(b_ref, sl) for sl in sls]
        k = [f32(k_ref, sl) for sl in sls]
        rh = [f32(r_ref, sl) * jnp.exp(lc[c]) for c, sl in enumerate(sls)]
        ah = [-f32(kk_ref, sl) * jnp.exp(lc[c] - ld[sl]) for c, sl in enumerate(sls)]
        p_inv = [jnp.exp(-x) for x in lc]
        p_end = [jnp.exp(l_end[c] - lc[c]) for c in chunks]
        bh = [b[c] * p_inv[c] for c in chunks]
        kh = [k[c] * p_inv[c] for c in chunks]
        b_end = [(b[c] * p_end[c]).astype(BF16) for c in chunks]
        k_end = [(k[c] * p_end[c]).astype(BF16) for c in chunks]
        yield
        v_bd = [bdb(x.astype(F32)) for x in v]
        lhs = [jnp.concatenate([ah[c], rh[c]], axis=0).astype(BF16) for c in chunks]
        gb = [_dot_nt(lhs[c], bdb(bh[c])) for c in chunks]
        gk = [_dot_nt(lhs[c], bdb(kh[c])) for c in chunks]
        a_ab = [jnp.where(strict, x[:CHUNK], 0.0) for x in gb]
        a_rb = [jnp.where(incl, x[CHUNK:], 0.0).astype(BF16) for x in gb]
        a_ak = [jnp.where(strict, x[:CHUNK], 0.0).astype(BF16) for x in gk]
        a_rk = [jnp.where(incl, x[CHUNK:], 0.0).astype(BF16) for x in gk]
        yield

        tinv = [eye_cat + x for x in a_ab]
        pw = [_dot(x.astype(BF16), bdb(x)) for x in a_ab]
        yield
        for _ in range(CHUNK.bit_length() - 3):
            both = [_dot(jnp.concatenate([pw[c], tinv[c]], axis=0).astype(BF16), bdb(pw[c]))
                    for c in chunks]
            tinv = [tinv[c] + both[c][CHUNK:] for c in chunks]
            pw = [x[:CHUNK] for x in both]
            yield
        tinv = [tinv[c] + _dot(tinv[c].astype(BF16), bdb(pw[c])) for c in chunks]
        yield

        bt = [_dot(a_rb[c], bdb(tinv[c])) for c in chunks]
        zk = [_dot(jnp.concatenate([a_ak[c], a_rk[c]], axis=0), v_bd[c]) for c in chunks]
        tbt = [jnp.concatenate([tinv[c], bt[c]], axis=0).astype(BF16) for c in chunks]
        yield
        wq = [_dot(tbt[c], bdb(ah[c])) for c in chunks]
        uy = [_dot(tbt[c], bdb(zk[c][:CHUNK])) for c in chunks]
        w = [x[:CHUNK] for x in wq]
        ut = [x[:CHUNK] for x in uy]
        for c, sl in enumerate(sls):
            qt_ref[ids[c]] = _head_transpose(rh[c] + wq[c][CHUNK:]).astype(qt_ref.dtype)
            y0_ref[sl, :] = uy[c][CHUNK:] + zk[c][CHUNK:]
        yield

        wu_t = [jnp.concatenate([_head_transpose(w[c]), _head_transpose(ut[c])],
                                axis=0).astype(BF16) for c in chunks]
        v_t = [_head_transpose(x.astype(F32)).astype(BF16) for x in v]
        for c in chunks:
            mn = _dot(wu_t[c], bdp(b_end[c]))
            nt_ref[ids[c]] = mn[HEAD_SIZE:] + _dot(v_t[c], bdp(k_end[c]))
            mt_ref[ids[c]] = (jnp.where(diag_cat, jnp.exp(l_end[c]), 0.0)
                              + mn[:HEAD_SIZE]).astype(mt_ref.dtype)

    _interleave([chain(list(range(tb // CHUNK)))])


def _scan_kernel(qt_ref, y0_ref, mt_ref, nt_ref, g_ref, bonus_ref, lnw_ref, lnb_ref, hmean_ref,
                 y_ref, ht_ref, yraw_ref):
    gl = GROUP_LANES
    nb, tb, _ = y0_ref.shape

    @pl.when(pl.program_id(1) == 0)
    def _():
        ht_ref[...] = jnp.zeros_like(ht_ref)

    bi = lax.broadcasted_iota(jnp.int32, (gl, gl), 0)
    bj = lax.broadcasted_iota(jnp.int32, (gl, gl), 1)
    bmask = (bi // HEAD_SIZE) == (bj // HEAD_SIZE)
    pmask = _perm_head(bi // HEAD_SIZE) == (bj // HEAD_SIZE)
    tile4 = lambda x: jnp.concatenate([x] * HEADS_PER_GROUP, axis=0)
    seqs = [(b, slice(gi * gl, (gi + 1) * gl)) for b in range(nb) for gi in range(N_GROUPS)]

    def normalize(rows):
        n = rows.stop - rows.start
        y = jnp.concatenate([yraw_ref[b, rows, gs] for b, gs in seqs], axis=0)
        d = y - _head_sums(y, hmean_ref[...])
        yn = d * lax.rsqrt(_head_sums(d * d, hmean_ref[...]) + GN_EPS)
        for i, (b, gs) in enumerate(seqs):
            out = yn[i * n:(i + 1) * n] * lnw_ref[:, gs] + lnb_ref[:, gs]
            y_ref[b, rows, gs] = ((out + bonus_ref[b, rows, gs])
                                  * g_ref[b, rows, gs]).astype(y_ref.dtype)

    ht = [ht_ref[b, :, gs] for b, gs in seqs]
    n_chunks = tb // CHUNK
    for c in range(n_chunks):
        sl = slice(c * CHUNK, (c + 1) * CHUNK)
        both = [_dot(ht[i].astype(BF16), jnp.concatenate(
            [jnp.where(bmask, tile4(mt_ref[b, c, :, gs]), 0.0),
             jnp.where(pmask, tile4(qt_ref[b, c, :, gs]), 0.0)], axis=1))
            for i, (b, gs) in enumerate(seqs)]
        for i, (b, gs) in enumerate(seqs):
            yraw_ref[b, sl, gs] = _head_transpose(both[i][:, gl:]) + y0_ref[b, sl, gs]
        ht = [both[i][:, :gl] + nt_ref[b, c, :, gs] for i, (b, gs) in enumerate(seqs)]
        if (c + 1) % SCAN_NORM_CHUNKS == 0 or c + 1 == n_chunks:
            first = c + 1 - ((c % SCAN_NORM_CHUNKS) + 1)
            normalize(slice(first * CHUNK, (c + 1) * CHUNK))
    for i, (b, gs) in enumerate(seqs):
        ht_ref[b, :, gs] = ht[i]


def _post_kernel(x1_ref, yrw_ref, ypool_ref, wout_ref, n2_ref, wg_ref, wu_ref, wd_ref, nf_ref,
                 o_ref):
    mix = _dot(yrw_ref[...].astype(BF16), wout_ref[0:RWKV_WIDTH, :])
    mix = mix + _dot(ypool_ref[...].astype(BF16), wout_ref[RWKV_WIDTH:, :])
    x2 = x1_ref[...] + mix
    h = _rms(x2, n2_ref[...]).astype(BF16)
    ffn = functools.reduce(lambda a, b: a + b, _swiglu_parts(h, wg_ref, wu_ref, wd_ref))
    o_ref[...] = _rms(x2 + 0.5 * ffn, nf_ref[...])


def _const_spec(shape):
    nd = len(shape)
    return pl.BlockSpec(shape, lambda *_: (0,) * nd, pipeline_mode=pl.Buffered(1))


def _head_block_ones(n, scale):
    i = jnp.arange(n) // HEAD_SIZE
    return jnp.where(i[:, None] == i[None, :], scale, 0.0).astype(BF16)


def kernel(x, ffn1_norm, ffn1_w_gate, ffn1_w_up, ffn1_w_down, mix_norm, w_in, mu_shift, w0, w_lora_up, a0, a_lora_up, g_lora_up, k_k, k_a, r_k, ln_w, ln_b, w_pool, pool_scale, w_out, ffn2_norm, ffn2_w_gate, ffn2_w_up, ffn2_w_down, final_norm):
    bsz, seq, _ = x.shape
    assert ffn1_norm.shape[0] == 1, "one trunk layer"
    tm = min(TOKEN_TILE, seq)
    tb = min(SCAN_TILE, seq)
    assert seq % tm == 0 and seq % tb == 0 and tb % CHUNK == 0 and tm >= POOL_HALO
    n_chunks = seq // CHUNK
    cpt = tb // CHUNK

    hsum = _head_block_ones(GROUP_LANES, 1.0)
    hmean = _head_block_ones(GROUP_LANES, 1.0 / HEAD_SIZE)
    row = lambda v: v.reshape(1, -1).astype(F32)

    tok = lambda width: pl.BlockSpec((None, tm, width), lambda b, t: (b, t, 0))
    tok_shape = lambda width, dtype=F32: jax.ShapeDtypeStruct((bsz, seq, width), dtype)

    w_in_p = jnp.concatenate(
        [w_in[0][:, :RWKV_COLS], jnp.zeros((D_MODEL, LORA_PAD - LORA_COLS), F32),
         w_in[0][:, RWKV_COLS:]], axis=1)
    mu_p = jnp.concatenate([mu_shift[0], jnp.zeros((LORA_PAD - LORA_COLS,), F32)])

    def lora_rows(w_up, start):
        pad = jnp.zeros((LORA_PAD, RWKV_WIDTH), F32)
        return pad.at[start:start + w_up.shape[0]].set(w_up).astype(BF16)

    for l in range(ffn1_norm.shape[0]):
        pre_in = [
            (x, tok(D_MODEL)),
            (row(ffn1_norm[l]), None),
            (ffn1_w_gate[l].astype(BF16), None),
            (ffn1_w_up[l].astype(BF16), None),
            (ffn1_w_down[l].astype(BF16), None),
            (row(mix_norm[l]), None),
            (w_in_p.astype(BF16), None),
            (row(mu_p), None),
            (row(w0[l]), None),
            (lora_rows(w_lora_up[l], 0), None),
            (row(a0[l]), None),
            (lora_rows(a_lora_up[l], D_DECAY_LORA), None),
            (lora_rows(g_lora_up[l], D_DECAY_LORA + D_AAA_LORA), None),
            (row(k_k[l]), None),
            (row(k_a[l]), None),
            (row(r_k[l]), None),
            (w_pool[l].astype(BF16), None),
            (row(pool_scale[l]), None),
            (hsum, None),
        ]
        pre_args = [a for a, _ in pre_in]
        pre_specs = [s if s is not None else _const_spec(a.shape) for a, s in pre_in]
        pre_dtypes = [BF16, BF16, BF16, BF16, BF16, F32, BF16, BF16, BF16]
        x1, r, k, v, kk, b, ld, g, bonus, ypool = pl.pallas_call(
            _pre_kernel,
            grid=(bsz, seq // tm),
            in_specs=pre_specs,
            out_specs=[tok(D_MODEL)] + [tok(RWKV_WIDTH)] * len(pre_dtypes),
            out_shape=[tok_shape(D_MODEL)] + [tok_shape(RWKV_WIDTH, d) for d in pre_dtypes],
            scratch_shapes=[pltpu.VMEM((8, SHIFT_COLS), F32),
                            pltpu.VMEM((POOL_HALO + tm, POOL_WIDTH), F32)],
            compiler_params=pltpu.CompilerParams(
                dimension_semantics=("arbitrary", "arbitrary"), vmem_limit_bytes=VMEM_LIMIT),
            name="pre",
        )(*pre_args)

        grp = pl.BlockSpec((None, tb, GROUP_LANES), lambda bb, gg, tt: (bb, tt, gg))
        mat = pl.BlockSpec((None, cpt, CHUNK, GROUP_LANES), lambda bb, gg, tt: (bb, tt, 0, gg))
        mat_shape = lambda dtype: jax.ShapeDtypeStruct((bsz, n_chunks, CHUNK, RWKV_WIDTH), dtype)
        q, y0, m, n = pl.pallas_call(
            _chunk_kernel,
            grid=(bsz, N_GROUPS, seq // tb),
            in_specs=[grp] * 6,
            out_specs=[mat, grp, mat, mat],
            out_shape=[mat_shape(BF16), tok_shape(RWKV_WIDTH), mat_shape(BF16), mat_shape(F32)],
            compiler_params=pltpu.CompilerParams(
                dimension_semantics=("arbitrary", "arbitrary", "arbitrary"),
                vmem_limit_bytes=VMEM_LIMIT),
            name="chunk",
        )(r, k, v, kk, b, ld)

        nb = SCAN_BATCH if bsz % SCAN_BATCH == 0 else 1
        tokb = pl.BlockSpec((nb, tb, RWKV_WIDTH), lambda bb, tt: (bb, tt, 0))
        matb = pl.BlockSpec((nb, cpt, CHUNK, RWKV_WIDTH), lambda bb, tt: (bb, tt, 0, 0))
        y_rw = pl.pallas_call(
            _scan_kernel,
            grid=(bsz // nb, seq // tb),
            in_specs=[matb, tokb, matb, matb, tokb, tokb,
                      _const_spec((1, RWKV_WIDTH)), _const_spec((1, RWKV_WIDTH)),
                      _const_spec((GROUP_LANES, GROUP_LANES))],
            out_specs=tokb,
            out_shape=tok_shape(RWKV_WIDTH, BF16),
            scratch_shapes=[pltpu.VMEM((nb, HEAD_SIZE, RWKV_WIDTH), F32),
                            pltpu.VMEM((nb, tb, RWKV_WIDTH), F32)],
            compiler_params=pltpu.CompilerParams(
                dimension_semantics=("arbitrary", "arbitrary"), vmem_limit_bytes=VMEM_LIMIT),
            name="scan",
        )(q, y0, m, n, g, bonus, row(ln_w[l]), row(ln_b[l]), hmean)

        post_in = [
            (x1, tok(D_MODEL)),
            (y_rw, tok(RWKV_WIDTH)),
            (ypool, tok(POOL_WIDTH)),
            (w_out[l].astype(BF16), None),
            (row(ffn2_norm[l]), None),
            (ffn2_w_gate[l].astype(BF16), None),
            (ffn2_w_up[l].astype(BF16), None),
            (ffn2_w_down[l].astype(BF16), None),
            (row(final_norm), None),
        ]
        post_args = [a for a, _ in post_in]
        post_specs = [s if s is not None else _const_spec(a.shape) for a, s in post_in]
        x = pl.pallas_call(
            _post_kernel,
            grid=(bsz, seq // tm),
            in_specs=post_specs,
            out_specs=tok(D_MODEL),
            out_shape=tok_shape(D_MODEL),
            compiler_params=pltpu.CompilerParams(
                dimension_semantics=("arbitrary", "arbitrary"), vmem_limit_bytes=VMEM_LIMIT),
            name="post",
        )(*post_args)
    return x
```

```python
import functools

import jax
import jax.numpy as jnp
from jax import lax
from jax.experimental import pallas as pl
from jax.experimental.pallas import tpu as pltpu

F32 = jnp.float32
BF16 = jnp.bfloat16

D_MODEL = 1024
RWKV_WIDTH = 512
POOL_WIDTH = 512
HEAD_SIZE = 64
D_DECAY_LORA = 32
D_AAA_LORA = 32
D_GATE_LORA = 96
LORA_COLS = D_DECAY_LORA + D_AAA_LORA + D_GATE_LORA
LORA_PAD = 256
POOL_WINDOWS = (2, 4, 8, 16)
POOL_GROUP = 128
POOL_HALO = 16
D_FF = 2816
RWKV_COLS = 3 * RWKV_WIDTH + LORA_COLS
SHIFT_COLS = 3 * RWKV_WIDTH + LORA_PAD
RMS_EPS = 1e-6
GN_EPS = 64e-5
L2_EPS = 1e-12

CHUNK = 64
GROUP_LANES = 256
HEADS_PER_GROUP = GROUP_LANES // HEAD_SIZE
N_GROUPS = RWKV_WIDTH // GROUP_LANES
MXU_TILE = 256
FF_CHUNKS = (6 * MXU_TILE, 5 * MXU_TILE)
assert sum(FF_CHUNKS) == D_FF

TOKEN_TILE = 512
PRE_BLOCKS = 4
PRE_BLOCK_LAG = 1
SCAN_TILE = 1024
SCAN_BATCH = 2
SCAN_NORM_CHUNKS = 8
CUMSUM_ROWS = 256
VMEM_LIMIT = 56 * 1024 * 1024


def _dot(a, b):
    return jnp.dot(a, b, preferred_element_type=F32)


def _dot_nt(a, b):
    return lax.dot_general(a, b, (((1,), (1,)), ((), ())), preferred_element_type=F32)


def _interleave(gens, lag=0):
    live = list(enumerate(gens))
    rnd = 0
    while live:
        for g, gen in list(live):
            if rnd >= g * lag and next(gen, StopIteration) is StopIteration:
                live.remove((g, gen))
        rnd += 1


def _head_sums(x, ones_bd):
    return _dot(x.astype(BF16), ones_bd)


def _rms(x, g):
    return x * lax.rsqrt(jnp.mean(x * x, axis=-1, keepdims=True) + RMS_EPS) * g


def _swiglu_parts(hb, wg_ref, wu_ref, wd_ref):
    c0 = 0
    for width in FF_CHUNKS:
        gate = _dot(hb, wg_ref[:, c0:c0 + width])
        up = _dot(hb, wu_ref[:, c0:c0 + width])
        act = (gate * jax.nn.sigmoid(gate) * up).astype(BF16)
        yield _dot(act, wd_ref[c0:c0 + width, :])
        c0 += width


def _pre_kernel(x_ref, n1_ref, wg_ref, wu_ref, wd_ref, nm_ref, wrkv_ref, wlora_ref, wpin_ref,
                mu_ref, w0_ref,
                wlu_ref, a0_ref, alu_ref, glu_ref, kk_ref, ka_ref, rk_ref, wpool_ref,
                pscale_ref, hsum_ref,
                x1_ref, r_ref, k_ref, v_ref, kkn_ref, b_ref, ld_ref, g_ref, bonus_ref,
                ypool_ref, carry_ref, pool_ref):
    tm = x_ref.shape[0]
    t = pl.program_id(1)

    @pl.when(t == 0)
    def _():
        carry_ref[...] = jnp.zeros_like(carry_ref)
        pool_ref[0:POOL_HALO, :] = jnp.zeros((POOL_HALO, POOL_WIDTH), F32)

    def block(rows):
        n = rows.stop - rows.start
        x = x_ref[rows, :]
        h = _rms(x, n1_ref[...]).astype(BF16)
        yield
        acc = None
        for part in _swiglu_parts(h, wg_ref, wu_ref, wd_ref):
            acc = part if acc is None else acc + part
            yield
        x1 = x + 0.5 * acc
        x1_ref[rows, :] = x1
        h2 = _rms(x1, nm_ref[...]).astype(BF16)
        p = jnp.concatenate([_dot(h2, wrkv_ref[...]), _dot(h2, wlora_ref[...]),
                             _dot(h2, wpin_ref[...])], axis=1)
        yield

        p_rw = p[:, :SHIFT_COLS]
        row = lax.broadcasted_iota(jnp.int32, (n, SHIFT_COLS), 0)
        prev = jnp.where(row == 0, carry_ref[0:1, :], pltpu.roll(p_rw, 1, 0))
        carry_ref[0:1, :] = p_rw[n - 1:n, :]
        p_rw = p_rw + (prev - p_rw) * mu_ref[...]

        p_r = p_rw[:, 0:RWKV_WIDTH]
        p_k = p_rw[:, RWKV_WIDTH:2 * RWKV_WIDTH]
        p_v = p_rw[:, 2 * RWKV_WIDTH:3 * RWKV_WIDTH]
        p_lora = p_rw[:, 3 * RWKV_WIDTH:SHIFT_COLS]

        hsum = hsum_ref[...]
        z = w0_ref[...] + _dot(jnp.tanh(p_lora).astype(BF16), wlu_ref[...])
        ld_ref[rows, :] = (-jnp.exp(-0.5)) * jax.nn.sigmoid(z)
        a = jax.nn.sigmoid(a0_ref[...] + _dot(p_lora.astype(BF16), alu_ref[...]))
        g_ref[rows, :] = _dot(jax.nn.sigmoid(p_lora).astype(BF16),
                              glu_ref[...]).astype(g_ref.dtype)

        kk = p_k * kk_ref[...]
        k = p_k * (1.0 + (a - 1.0) * ka_ref[...])
        rkr = p_r * k * rk_ref[...]
        r_ref[rows, :] = p_r.astype(r_ref.dtype)
        k_ref[rows, :] = k.astype(k_ref.dtype)
        v_ref[rows, :] = p_v.astype(v_ref.dtype)
        yield
        groups = [slice(gi * GROUP_LANES, (gi + 1) * GROUP_LANES) for gi in range(N_GROUPS)]
        sums = _head_sums(jnp.concatenate([kk[:, gs] * kk[:, gs] for gs in groups]
                                          + [rkr[:, gs] for gs in groups], axis=0), hsum)
        for gi, gs in enumerate(groups):
            ss = sums[gi * n:(gi + 1) * n]
            rk_sum = sums[(N_GROUPS + gi) * n:(N_GROUPS + gi + 1) * n]
            kk_g = kk[:, gs] / jnp.maximum(jnp.sqrt(ss), L2_EPS)
            kkn_ref[rows, gs] = kk_g.astype(kkn_ref.dtype)
            b_ref[rows, gs] = (kk_g * a[:, gs]).astype(b_ref.dtype)
            bonus_ref[rows, gs] = (rk_sum * p_v[:, gs]).astype(bonus_ref.dtype)
        yield

        base = POOL_HALO + rows.start
        pool_ref[base:base + n, :] = p[:, SHIFT_COLS:]
        pos = t * tm + rows.start + lax.broadcasted_iota(jnp.int32, (n, POOL_GROUP), 0) + 1
        for gi, win in enumerate(POOL_WINDOWS):
            cols = slice(gi * POOL_GROUP, (gi + 1) * POOL_GROUP)
            cur = pool_ref[base:base + n, cols]
            wsum = cur
            for j in range(1, win):
                wsum = wsum + pool_ref[base - j:base - j + n, cols]
            count = jnp.minimum(pos, win).astype(F32)
            pooled = wsum / count - cur
            mixed = _dot(pooled.astype(BF16), wpool_ref[gi])
            ypool_ref[rows, cols] = (mixed * pscale_ref[:, cols]).astype(ypool_ref.dtype)

    step = tm // PRE_BLOCKS
    _interleave([block(slice(i, i + step)) for i in range(0, tm, step)], PRE_BLOCK_LAG)
    pool_ref[0:POOL_HALO, :] = pool_ref[tm:tm + POOL_HALO, :]


def _block_diag(x, bmask):
    return jnp.where(bmask, jnp.concatenate([x] * HEADS_PER_GROUP, axis=0), 0.0)


def _head_transpose(x):
    half = GROUP_LANES // 2
    y = jnp.concatenate([x[:, :half], x[:, half:]], axis=0)
    yt = y.T
    return jnp.concatenate([yt[:HEAD_SIZE], yt[HEAD_SIZE:]], axis=1)


def _perm_head(block):
    return (block % 2) * 2 + block // 2


def _chunk_kernel(r_ref, k_ref, v_ref, kk_ref, b_ref, ld_ref, qt_ref, y0_ref, mt_ref, nt_ref):
    tb = r_ref.shape[0]
    gl = GROUP_LANES

    cb = min(tb, CUMSUM_ROWS)
    ri = lax.broadcasted_iota(jnp.int32, (cb, cb), 0)
    ci = lax.broadcasted_iota(jnp.int32, (cb, cb), 1)
    tri = jnp.where((ri // CHUNK == ci // CHUNK) & (ci <= ri), 1.0, 0.0).astype(BF16)
    ld16 = ld_ref[...].astype(BF16)
    ld = ld16.astype(F32)
    cum = jnp.concatenate([_dot(tri, ld16[i:i + cb]) for i in range(0, tb, cb)], axis=0)

    bi = lax.broadcasted_iota(jnp.int32, (gl, gl), 0)
    bj = lax.broadcasted_iota(jnp.int32, (gl, gl), 1)
    bmask = (bi // HEAD_SIZE) == (bj // HEAD_SIZE)
    trow = lax.broadcasted_iota(jnp.int32, (CHUNK, gl), 0)
    scol = lax.broadcasted_iota(jnp.int32, (CHUNK, gl), 1) % CHUNK
    strict = scol < trow
    incl = scol <= trow
    diag_cat = scol == trow
    eye_cat = jnp.where(diag_cat, 1.0, 0.0)
    bdb = lambda x: _block_diag(x, bmask).astype(BF16)
    pmask =_perm_head(bi // HEAD_SIZE) == (bj // HEAD_SIZE)
    bdp = lambda x: jnp.where(pmask, jnp.concatenate([x] * HEADS_PER_GROUP, axis=0), 0.0)
    f32 = lambda ref, sl: ref[sl, :].astype(F32)

    def chain(ids):
        chunks = range(len(ids))
        sls = [slice(c * CHUNK, (c + 1) * CHUNK) for c in ids]
        lc = [cum[sl] for sl in sls]
        l_end = [x[CHUNK - 1:CHUNK, :] for x in lc]
        v = [v_ref[sl, :] for sl in sls]
        b = [f32(b_ref, sl) for sl in sls]
        k = [f32(k_ref, sl) for sl in sls]
        rh = [f32(r_ref, sl) * jnp.exp(lc[c]) for c, sl in enumerate(sls)]
        ah = [-f32(kk_ref, sl) * jnp.exp(lc[c] - ld[sl]) for c, sl in enumerate(sls)]
        p_inv = [jnp.exp(-x) for x in lc]
        p_end = [jnp.exp(l_end[c] - lc[c]) for c in chunks]
        bh = [b[c] * p_inv[c] for c in chunks]
        kh = [k[c] * p_inv[c] for c in chunks]
        b_end = [(b[c] * p_end[c]).astype(BF16) for c in chunks]
        k_end = [(k[c] * p_end[c]).astype(BF16) for c in chunks]
        yield
        v_bd = [bdb(x.astype(F32)) for x in v]
        lhs = [jnp.concatenate([ah[c], rh[c]], axis=0).astype(BF16) for c in chunks]
        gb = [_dot_nt(lhs[c], bdb(bh[c])) for c in chunks]
        gk = [_dot_nt(lhs[c], bdb(kh[c])) for c in chunks]
        a_ab = [jnp.where(strict, x[:CHUNK], 0.0) for x in gb]
        a_rb = [jnp.where(incl, x[CHUNK:], 0.0).astype(BF16) for x in gb]
        a_ak = [jnp.where(strict, x[:CHUNK], 0.0).astype(BF16) for x in gk]
        a_rk = [jnp.where(incl, x[CHUNK:], 0.0).astype(BF16) for x in gk]
        yield

        tinv = [eye_cat + x for x in a_ab]
        pw = [_dot(x.astype(BF16), bdb(x)) for x in a_ab]
        yield
        for _ in range(CHUNK.bit_length() - 3):
            both = [_dot(jnp.concatenate([pw[c], tinv[c]], axis=0).astype(BF16), bdb(pw[c]))
                    for c in chunks]
            tinv = [tinv[c] + both[c][CHUNK:] for c in chunks]
            pw = [x[:CHUNK] for x in both]
            yield
        tinv = [tinv[c] + _dot(tinv[c].astype(BF16), bdb(pw[c])) for c in chunks]
        yield

        bt = [_dot(a_rb[c], bdb(tinv[c])) for c in chunks]
        zk = [_dot(jnp.concatenate([a_ak[c], a_rk[c]], axis=0), v_bd[c]) for c in chunks]
        tbt = [jnp.concatenate([tinv[c], bt[c]], axis=0).astype(BF16) for c in chunks]
        yield
        wq = [_dot(tbt[c], bdb(ah[c])) for c in chunks]
        uy = [_dot(tbt[c], bdb(zk[c][:CHUNK])) for c in chunks]
        w = [x[:CHUNK] for x in wq]
        ut = [x[:CHUNK] for x in uy]
        for c, sl in enumerate(sls):
            qt_ref[ids[c]] = _head_transpose(rh[c] + wq[c][CHUNK:]).astype(qt_ref.dtype)
            y0_ref[sl, :] = uy[c][CHUNK:] + zk[c][CHUNK:]
        yield

        wu_t = [jnp.concatenate([_head_transpose(w[c]), _head_transpose(ut[c])],
                                axis=0).astype(BF16) for c in chunks]
        v_t = [_head_transpose(x.astype(F32)).astype(BF16) for x in v]
        for c in chunks:
            mn = _dot(wu_t[c], bdp(b_end[c]))
            nt_ref[ids[c]] = mn[HEAD_SIZE:] + _dot(v_t[c], bdp(k_end[c]))
            mt_ref[ids[c]] = (jnp.where(diag_cat, jnp.exp(l_end[c]), 0.0)
                              + mn[:HEAD_SIZE]).astype(mt_ref.dtype)

    _interleave([chain(list(range(tb // CHUNK)))])


def _scan_kernel(qt_ref, y0_ref, mt_ref, nt_ref, g_ref, bonus_ref, lnw_ref, lnb_ref, hmean_ref,
                 y_ref, ht_ref, yraw_ref):
    gl = GROUP_LANES
    nb, tb, _ = y0_ref.shape

    @pl.when(pl.program_id(1) == 0)
    def _():
        ht_ref[...] = jnp.zeros_like(ht_ref)

    bi = lax.broadcasted_iota(jnp.int32, (gl, gl), 0)
    bj = lax.broadcasted_iota(jnp.int32, (gl, gl), 1)
    bmask = (bi // HEAD_SIZE) == (bj // HEAD_SIZE)
    pmask = _perm_head(bi // HEAD_SIZE) == (bj // HEAD_SIZE)
    tile4 = lambda x: jnp.concatenate([x] * HEADS_PER_GROUP, axis=0)
    seqs = [(b, slice(gi * gl, (gi + 1) * gl)) for b in range(nb) for gi in range(N_GROUPS)]

    def normalize(rows):
        n = rows.stop - rows.start
        y = jnp.concatenate([yraw_ref[b, rows, gs] for b, gs in seqs], axis=0)
        d = y - _head_sums(y, hmean_ref[...])
        yn = d * lax.rsqrt(_head_sums(d * d, hmean_ref[...]) + GN_EPS)
        for i, (b, gs) in enumerate(seqs):
            out = yn[i * n:(i + 1) * n] * lnw_ref[:, gs] + lnb_ref[:, gs]
            y_ref[b, rows, gs] = ((out + bonus_ref[b, rows, gs])
                                  * g_ref[b, rows, gs]).astype(y_ref.dtype)

    ht = [ht_ref[b, :, gs] for b, gs in seqs]
    n_chunks = tb // CHUNK
    for c in range(n_chunks):
        sl = slice(c * CHUNK, (c + 1) * CHUNK)
        both = [_dot(ht[i].astype(BF16), jnp.concatenate(
            [jnp.where(bmask, tile4(mt_ref[b, c, :, gs]), 0.0),
             jnp.where(pmask, tile4(qt_ref[b, c, :, gs]), 0.0)], axis=1))
            for i, (b, gs) in enumerate(seqs)]
        for i, (b, gs) in enumerate(seqs):
            yraw_ref[b, sl, gs] = _head_transpose(both[i][:, gl:]) + y0_ref[b, sl, gs]
        ht = [both[i][:, :gl] + nt_ref[b, c, :, gs] for i, (b, gs) in enumerate(seqs)]
        if (c + 1) % SCAN_NORM_CHUNKS == 0 or c + 1 == n_chunks:
            first = c + 1 - ((c % SCAN_NORM_CHUNKS) + 1)
            normalize(slice(first * CHUNK, (c + 1) * CHUNK))
    for i, (b, gs) in enumerate(seqs):
        ht_ref[b, :, gs] = ht[i]


def _post_kernel(x1_ref, yrw_ref, ypool_ref, wout_ref, n2_ref, wg_ref, wu_ref, wd_ref, nf_ref,
                 o_ref):
    mix = _dot(yrw_ref[...].astype(BF16), wout_ref[0:RWKV_WIDTH, :])
    mix = mix + _dot(ypool_ref[...].astype(BF16), wout_ref[RWKV_WIDTH:, :])
    x2 = x1_ref[...] + mix
    h = _rms(x2, n2_ref[...]).astype(BF16)
    ffn = functools.reduce(lambda a, b: a + b, _swiglu_parts(h, wg_ref, wu_ref, wd_ref))
    o_ref[...] = _rms(x2 + 0.5 * ffn, nf_ref[...])


def _const_spec(shape):
    nd = len(shape)
    return pl.BlockSpec(shape, lambda *_: (0,) * nd, pipeline_mode=pl.Buffered(1))


def _head_block_ones(n, scale):
    i = jnp.arange(n) // HEAD_SIZE
    return jnp.where(i[:, None] == i[None, :], scale, 0.0).astype(BF16)


def kernel(x, ffn1_norm, ffn1_w_gate, ffn1_w_up, ffn1_w_down, mix_norm, w_in, mu_shift, w0, w_lora_up, a0, a_lora_up, g_lora_up, k_k, k_a, r_k, ln_w, ln_b, w_pool, pool_scale, w_out, ffn2_norm, ffn2_w_gate, ffn2_w_up, ffn2_w_down, final_norm):
    bsz, seq, _ = x.shape
    assert ffn1_norm.shape[0] == 1, "one trunk layer"
    tm = min(TOKEN_TILE, seq)
    tb = min(SCAN_TILE, seq)
    assert seq % tm == 0 and seq % tb == 0 and tb % CHUNK == 0 and tm >= POOL_HALO
    n_chunks = seq // CHUNK
    cpt = tb // CHUNK

    hsum = _head_block_ones(GROUP_LANES, 1.0)
    hmean = _head_block_ones(GROUP_LANES, 1.0 / HEAD_SIZE)
    row = lambda v: v.reshape(1, -1).astype(F32)

    tok = lambda width: pl.BlockSpec((None, tm, width), lambda b, t: (b, t, 0))
    tok_shape = lambda width, dtype=F32: jax.ShapeDtypeStruct((bsz, seq, width), dtype)

    rkv_cols = 3 * RWKV_WIDTH
    w_rkv = w_in[0][:, :rkv_cols].astype(BF16)
    w_lora = jnp.pad(w_in[0][:, rkv_cols:RWKV_COLS].astype(BF16),
                     ((0, 0), (0, LORA_PAD - LORA_COLS)))
    w_pool_in = w_in[0][:, RWKV_COLS:].astype(BF16)
    mu_p = jnp.concatenate([mu_shift[0], jnp.zeros((LORA_PAD - LORA_COLS,), F32)])

    def lora_rows(w_up, start):
        pad = jnp.zeros((LORA_PAD, RWKV_WIDTH), F32)
        return pad.at[start:start + w_up.shape[0]].set(w_up).astype(BF16)

    for l in range(ffn1_norm.shape[0]):
        pre_in = [
            (x, tok(D_MODEL)),
            (row(ffn1_norm[l]), None),
            (ffn1_w_gate[l].astype(BF16), None),
            (ffn1_w_up[l].astype(BF16), None),
            (ffn1_w_down[l].astype(BF16), None),
            (row(mix_norm[l]), None),
            (w_rkv, None),
            (w_lora, None),
            (w_pool_in, None),
            (row(mu_p), None),
            (row(w0[l]), None),
            (lora_rows(w_lora_up[l], 0), None),
            (row(a0[l]), None),
            (lora_rows(a_lora_up[l], D_DECAY_LORA), None),
            (lora_rows(g_lora_up[l], D_DECAY_LORA + D_AAA_LORA), None),
            (row(k_k[l]), None),
            (row(k_a[l]), None),
            (row(r_k[l]), None),
            (w_pool[l].astype(BF16), None),
            (row(pool_scale[l]), None),
            (hsum, None),
        ]
        pre_args = [a for a, _ in pre_in]
        pre_specs = [s if s is not None else _const_spec(a.shape) for a, s in pre_in]
        pre_dtypes = [BF16, BF16, BF16, BF16, BF16, F32, BF16, BF16, BF16]
        x1, r, k, v, kk, b, ld, g, bonus, ypool = pl.pallas_call(
            _pre_kernel,
            grid=(bsz, seq // tm),
            in_specs=pre_specs,
            out_specs=[tok(D_MODEL)] + [tok(RWKV_WIDTH)] * len(pre_dtypes),
            out_shape=[tok_shape(D_MODEL)] + [tok_shape(RWKV_WIDTH, d) for d in pre_dtypes],
            scratch_shapes=[pltpu.VMEM((8, SHIFT_COLS), F32),
                            pltpu.VMEM((POOL_HALO + tm, POOL_WIDTH), F32)],
            compiler_params=pltpu.CompilerParams(
                dimension_semantics=("arbitrary", "arbitrary"), vmem_limit_bytes=VMEM_LIMIT),
            name="pre",
        )(*pre_args)

        grp = pl.BlockSpec((None, tb, GROUP_LANES), lambda bb, gg, tt: (bb, tt, gg))
        mat = pl.BlockSpec((None, cpt, CHUNK, GROUP_LANES), lambda bb, gg, tt: (bb, tt, 0, gg))
        mat_shape = lambda dtype: jax.ShapeDtypeStruct((bsz, n_chunks, CHUNK, RWKV_WIDTH), dtype)
        q, y0, m, n = pl.pallas_call(
            _chunk_kernel,
            grid=(bsz, N_GROUPS, seq // tb),
            in_specs=[grp] * 6,
            out_specs=[mat, grp, mat, mat],
            out_shape=[mat_shape(BF16), tok_shape(RWKV_WIDTH), mat_shape(BF16), mat_shape(F32)],
            compiler_params=pltpu.CompilerParams(
                dimension_semantics=("arbitrary", "arbitrary", "arbitrary"),
                vmem_limit_bytes=VMEM_LIMIT),
            name="chunk",
        )(r, k, v, kk, b, ld)

        nb = SCAN_BATCH if bsz % SCAN_BATCH == 0 else 1
        tokb = pl.BlockSpec((nb, tb, RWKV_WIDTH), lambda bb, tt: (bb, tt, 0))
        matb = pl.BlockSpec((nb, cpt, CHUNK, RWKV_WIDTH), lambda bb, tt: (bb, tt, 0, 0))
        y_rw = pl.pallas_call(
            _scan_kernel,
            grid=(bsz // nb, seq // tb),
            in_specs=[matb, tokb, matb, matb, tokb, tokb,
                      _const_spec((1, RWKV_WIDTH)), _const_spec((1, RWKV_WIDTH)),
                      _const_spec((GROUP_LANES, GROUP_LANES))],
            out_specs=tokb,
            out_shape=tok_shape(RWKV_WIDTH, BF16),
            scratch_shapes=[pltpu.VMEM((nb, HEAD_SIZE, RWKV_WIDTH), F32),
                            pltpu.VMEM((nb, tb, RWKV_WIDTH), F32)],
            compiler_params=pltpu.CompilerParams(
                dimension_semantics=("arbitrary", "arbitrary"), vmem_limit_bytes=VMEM_LIMIT),
            name="scan",
        )(q, y0, m, n, g, bonus, row(ln_w[l]), row(ln_b[l]), hmean)

        post_in = [
            (x1, tok(D_MODEL)),
            (y_rw, tok(RWKV_WIDTH)),
            (ypool, tok(POOL_WIDTH)),
            (w_out[l].astype(BF16), None),
            (row(ffn2_norm[l]), None),
            (ffn2_w_gate[l].astype(BF16), None),
            (ffn2_w_up[l].astype(BF16), None),
            (ffn2_w_down[l].astype(BF16), None),
            (row(final_norm), None),
        ]
        post_args = [a for a, _ in post_in]
        post_specs = [s if s is not None else _const_spec(a.shape) for a, s in post_in]
        x = pl.pallas_call(
            _post_kernel,
            grid=(bsz, seq // tm),
            in_specs=post_specs,
            out_specs=tok(D_MODEL),
            out_shape=tok_shape(D_MODEL),
            compiler_params=pltpu.CompilerParams(
                dimension_semantics=("arbitrary", "arbitrary"), vmem_limit_bytes=VMEM_LIMIT),
            name="post",
        )(*post_args)
    return x
```
